```python
import math
import jax
import jax.numpy as jnp
from jax import lax
import numpy as np

D_MODEL = 1024
BATCH = 16
SEQ = 2048
DEPTH = 1

HEAD_DIM = 64
N_HEADS_SB = 8
N_HEADS_DIL = 8
D_SB = N_HEADS_SB * HEAD_DIM
D_DIL = N_HEADS_DIL * HEAD_DIM
D_MIX = D_SB + D_DIL
D_FF = 2816
DIL_CONFIGS = ((128, 1), (512, 4), (2048, 16))
BLOCK = 128
N_BUCKETS = 32
MAX_DISTANCE = 2048
N_MOD = 9
EPS = 1e-6
NEG_INF = -1e30

kernel_name = 'hybrid_stickbreak_dilated_macaron_adaln'


def rmsnorm(x, g):
    x32 = x.astype(jnp.float32)
    y = x32 * lax.rsqrt(jnp.mean(x32 * x32, axis=-1, keepdims=True) + EPS)
    return y.astype(x.dtype) * g


def modulate(x, g, shift, scale):
    return rmsnorm(x, g) * (1 + scale[:, None, :]) + shift[:, None, :]


def swiglu(h, w_gate, w_up, w_down):
    return (jax.nn.silu(h @ w_gate) * (h @ w_up)) @ w_down


def to_heads(a, n_heads):
    b, s, _ = a.shape
    return a.reshape(b, s, n_heads, HEAD_DIM).transpose(0, 2, 1, 3)


def from_heads(a):
    b, h, s, d = a.shape
    return a.transpose(0, 2, 1, 3).reshape(b, s, h * d)


def t5_causal_bucket(n):
    max_exact = N_BUCKETS // 2
    nf = np.maximum(n, 1).astype(np.float32)
    large = max_exact + (np.log(nf / max_exact) / math.log(MAX_DISTANCE / max_exact)
                         * (N_BUCKETS - max_exact)).astype(np.int32)
    large = np.minimum(large, N_BUCKETS - 1)
    return np.where(n < max_exact, n, large).astype(np.int32)


def stick_breaking_attention(q, k, v):
    b, h, s, dh = q.shape
    n_blk = s // BLOCK
    scale = dh ** -0.5
    key_pos = jnp.arange(s)

    def one_block(blk):
        qb = lax.dynamic_slice_in_dim(q, blk * BLOCK, BLOCK, axis=2)
        z = jnp.einsum('bhtd,bhsd->bhts', qb, k).astype(jnp.float32) * scale
        q_pos = blk * BLOCK + jnp.arange(BLOCK)
        causal = key_pos[None, :] < q_pos[:, None]
        log_not = jnp.where(causal, jax.nn.log_sigmoid(-z), 0.0)
        suffix = lax.cumsum(log_not, axis=3, reverse=True) - log_not
        w = jnp.where(causal, jnp.exp(jax.nn.log_sigmoid(z) + suffix), 0.0)
        return jnp.einsum('bhts,bhsd->bhtd', w.astype(v.dtype), v)

    out = lax.map(one_block, jnp.arange(n_blk))
    return out.transpose(1, 2, 0, 3, 4).reshape(b, h, s, dh)


def dilated_config(q, k, v, bias_table, window, dilation):
    b, h, s, dh = q.shape
    n_steps = window // dilation
    sub_len = s // dilation
    n_blk = -(-sub_len // BLOCK)
    pad = n_blk * BLOCK - sub_len

    def to_residue(a):
        a = a.reshape(b, h, sub_len, dilation, dh).transpose(0, 1, 3, 2, 4)
        a = jnp.pad(a, ((0, 0), (0, 0), (0, 0), (0, pad), (0, 0)))
        return a.reshape(b, h, dilation, n_blk, BLOCK, dh)

    def with_prev(a):
        prev = jnp.pad(a, ((0, 0), (0, 0), (0, 0), (1, 0), (0, 0), (0, 0)))[:, :, :, :-1]
        return jnp.concatenate([prev, a], axis=4)

    qr = to_residue(q)
    kb = with_prev(to_residue(k))
    vb = with_prev(to_residue(v))
    z = jnp.einsum('bhrnqd,bhrnkd->bhrnqk', qr, kb).astype(jnp.float32) * (dh ** -0.5)

    step = BLOCK + np.arange(BLOCK)[:, None] - np.arange(2 * BLOCK)[None, :]
    in_band = (step >= 0) & (step <= n_steps)
    has_prev = (np.arange(n_blk)[:, None, None] > 0) | (np.arange(2 * BLOCK)[None, None, :] >= BLOCK)
    valid = in_band[None] & has_prev

    bias = bias_table[t5_causal_bucket(np.arange(n_steps + 1) * dilation)]
    bias = bias[np.clip(step, 0, n_steps)].astype(jnp.float32)
    z = z + bias.transpose(2, 0, 1)[None, :, None, None]
    z = jnp.where(valid[None, None, None], z, NEG_INF)

    m = jnp.max(z, axis=-1, keepdims=True)
    e = jnp.exp(z - m)
    denom = jnp.sum(e, axis=-1)
    o = jnp.einsum('bhrnqk,bhrnkd->bhrnqd', e.astype(v.dtype), vb) / denom[..., None].astype(v.dtype)
    lse = m[..., 0] + jnp.log(denom)

    def from_residue(a):
        a = a.reshape(b, h, dilation, n_blk * BLOCK, a.shape[-1])[:, :, :, :sub_len]
        return a.transpose(0, 1, 3, 2, 4).reshape(b, h, s, a.shape[-1])

    return from_residue(o), from_residue(lse[..., None])[..., 0]


def dilated_attention(q, k, v, bias_table):
    outs, lses = [], []
    for window, dilation in DIL_CONFIGS:
        o, lse = dilated_config(q, k, v, bias_table, window, dilation)
        outs.append(o)
        lses.append(lse)
    alpha = jax.nn.softmax(jnp.stack(lses, axis=0), axis=0)
    return jnp.einsum('cbhs,cbhsd->bhsd', alpha.astype(q.dtype), jnp.stack(outs, axis=0))


def token_mixer(h, w_in, g_sb_out, g_dil_out, w_out, rel_bias):
    qkv = h @ w_in
    q_sb, k_sb, v_sb, q_dil, k_dil, v_dil = jnp.split(
        qkv, [D_SB, 2 * D_SB, 3 * D_SB, 3 * D_SB + D_DIL, 3 * D_SB + 2 * D_DIL], axis=-1)
    o_sb = stick_breaking_attention(to_heads(q_sb, N_HEADS_SB), to_heads(k_sb, N_HEADS_SB),
                                    to_heads(v_sb, N_HEADS_SB))
    o_dil = dilated_attention(to_heads(q_dil, N_HEADS_DIL), to_heads(k_dil, N_HEADS_DIL),
                              to_heads(v_dil, N_HEADS_DIL), rel_bias)
    o_sb = rmsnorm(o_sb, g_sb_out[:, None, :])
    o_dil = rmsnorm(o_dil, g_dil_out[:, None, :])
    o = jnp.concatenate([from_heads(o_sb), from_heads(o_dil)], axis=-1)
    return o @ w_out


def setup_inputs(seed: int = 0) -> dict:
    key = jax.random.key(seed)
    ks = jax.random.split(key, 20)
    f32 = jnp.float32

    def nrm(k, shape, scale):
        return jax.random.normal(k, shape, f32) * scale

    def gain(k, shape):
        return 1.0 + 0.05 * jax.random.normal(k, shape, f32)

    L, D = DEPTH, D_MODEL
    return {
        'x': nrm(ks[0], (BATCH, SEQ, D), 1.0),
        'c': nrm(ks[1], (BATCH, D), 1.0),
        'w_ada': nrm(ks[2], (L, D, N_MOD * D), 0.5 * D ** -0.5),
        'b_ada': nrm(ks[3], (L, N_MOD * D), 0.02),
        'g_ffn1': gain(ks[4], (L, D)),
        'w1_gate': nrm(ks[5], (L, D, D_FF), D ** -0.5),
        'w1_up': nrm(ks[6], (L, D, D_FF), D ** -0.5),
        'w1_down': nrm(ks[7], (L, D_FF, D), D_FF ** -0.5),
        'g_mix': gain(ks[8], (L, D)),
        'w_in': nrm(ks[9], (L, D, 3 * D_MIX), D ** -0.5),
        'g_sb_out': gain(ks[10], (L, N_HEADS_SB, HEAD_DIM)),
        'g_dil_out': gain(ks[11], (L, N_HEADS_DIL, HEAD_DIM)),
        'w_out': nrm(ks[12], (L, D_MIX, D), D_MIX ** -0.5),
        'rel_bias': nrm(ks[13], (N_BUCKETS, N_HEADS_DIL), 0.5),
        'g_ffn2': gain(ks[14], (L, D)),
        'w2_gate': nrm(ks[15], (L, D, D_FF), D ** -0.5),
        'w2_up': nrm(ks[16], (L, D, D_FF), D ** -0.5),
        'w2_down': nrm(ks[17], (L, D_FF, D), D_FF ** -0.5),
        'g_final': gain(ks[18], (D,)),
    }


def reference(x, c, w_ada, b_ada, g_ffn1, w1_gate, w1_up, w1_down, g_mix, w_in, g_sb_out,
              g_dil_out, w_out, rel_bias, g_ffn2, w2_gate, w2_up, w2_down, g_final):
    for l in range(DEPTH):
        mod = (jax.nn.silu(c) @ w_ada[l] + b_ada[l]).reshape(c.shape[0], N_MOD, D_MODEL)
        sh1, sc1, gt1, sh2, sc2, gt2, sh3, sc3, gt3 = [mod[:, i] for i in range(N_MOD)]
        h = modulate(x, g_ffn1[l], sh1, sc1)
        x = x + 0.5 * gt1[:, None, :] * swiglu(h, w1_gate[l], w1_up[l], w1_down[l])
        h = modulate(x, g_mix[l], sh2, sc2)
        x = x + gt2[:, None, :] * token_mixer(h, w_in[l], g_sb_out[l], g_dil_out[l], w_out[l], rel_bias)
        h = modulate(x, g_ffn2[l], sh3, sc3)
        x = x + 0.5 * gt3[:, None, :] * swiglu(h, w2_gate[l], w2_up[l], w2_down[l])
    return rmsnorm(x, g_final)
```

```python
import functools
import math

import jax
import jax.numpy as jnp
import numpy as np
from jax import lax
from jax.experimental import pallas as pl
from jax.experimental.pallas import tpu as pltpu

F32 = jnp.float32
BF16 = jnp.bfloat16

HEAD_DIM = 64
DIL_CONFIGS = ((128, 1), (512, 4), (2048, 16))
MAX_DISTANCE = 2048
N_MOD = 9
EPS = 1e-6
NEG_INF = -1e30

ATT_TILE = 256
FFN_TOKEN_TILE = 512
VMEM_LIMIT = 56 * 1024 * 1024


def _cparams(n_grid):
    return pltpu.CompilerParams(dimension_semantics=("arbitrary",) * n_grid, vmem_limit_bytes=VMEM_LIMIT)


def _resident(block_shape, index_map):
    return pl.BlockSpec(block_shape, index_map, pipeline_mode=pl.Buffered(1))


def _mod_kernel(c_ref, w_ref, b_ref, o_ref):
    c = c_ref[...]
    s = c * jax.nn.sigmoid(c)
    o_ref[...] = jnp.dot(s.astype(BF16), w_ref[...].astype(BF16), preferred_element_type=F32) + b_ref[...]


def _adaln_mod(c, w_ada, b_ada):
    b, d = c.shape
    n = w_ada.shape[1]
    tn = 1024
    out = pl.pallas_call(
        _mod_kernel,
        grid=(n // tn,),
        in_specs=[
            pl.BlockSpec((b, d), lambda j: (0, 0)),
            pl.BlockSpec((d, tn), lambda j: (0, j)),
            pl.BlockSpec((1, tn), lambda j: (0, j)),
        ],
        out_specs=pl.BlockSpec((b, tn), lambda j: (0, j)),
        out_shape=jax.ShapeDtypeStruct((b, n), F32),
        compiler_params=_cparams(1),
        name="adaln_mod",
    )(c, w_ada, b_ada.reshape(1, n))
    return out.reshape(b, N_MOD, d)


def _modulate(x, g, shift, scale):
    y = x * lax.rsqrt(jnp.mean(x * x, axis=-1, keepdims=True) + EPS)
    return (y * g) * (1.0 + scale) + shift


def _ffn_kernel(*refs, mod_base, ff_tile, final):
    if final:
        x_ref, mod_ref, g_ref, wg_ref, wu_ref, wd_ref, gf_ref, o_ref = refs
    else:
        x_ref, mod_ref, g_ref, wg_ref, wu_ref, wd_ref, o_ref = refs
    x = x_ref[0]
    shift = mod_ref[0, mod_base:mod_base + 1, :]
    scale = mod_ref[0, mod_base + 1:mod_base + 2, :]
    gate = mod_ref[0, mod_base + 2:mod_base + 3, :]
    hb = _modulate(x, g_ref[...], shift, scale).astype(BF16)
    acc = None
    for j in range(wg_ref.shape[1] // ff_tile):
        sl = slice(j * ff_tile, (j + 1) * ff_tile)
        gg = jnp.dot(hb, wg_ref[:, sl], preferred_element_type=F32)
        uu = jnp.dot(hb, wu_ref[:, sl], preferred_element_type=F32)
        a = ((gg * jax.nn.sigmoid(gg)) * uu).astype(BF16)
        dd = jnp.dot(a, wd_ref[sl, :], preferred_element_type=F32)
        acc = dd if acc is None else acc + dd
    out = x + (0.5 * gate) * acc
    if final:
        out = out * lax.rsqrt(jnp.mean(out * out, axis=-1, keepdims=True) + EPS) * gf_ref[...]
    o_ref[0] = out


def _ffn(x, mod, g, wg, wu, wd, mod_base, g_final=None):
    b, s, d = x.shape
    dff = wg.shape[1]
    tm = FFN_TOKEN_TILE
    ff_tile = dff // 2
    final = g_final is not None
    in_specs = [
        pl.BlockSpec((1, tm, d), lambda i, j: (i, j, 0)),
        pl.BlockSpec((1, N_MOD, d), lambda i, j: (i, 0, 0)),
        _resident((1, d), lambda i, j: (0, 0)),
        _resident((d, dff), lambda i, j: (0, 0)),
        _resident((d, dff), lambda i, j: (0, 0)),
        _resident((dff, d), lambda i, j: (0, 0)),
    ]
    args = [x, mod, g.reshape(1, d), wg, wu, wd]
    if final:
        in_specs.append(_resident((1, d), lambda i, j: (0, 0)))
        args.append(g_final.reshape(1, d))
    return pl.pallas_call(
        functools.partial(_ffn_kernel, mod_base=mod_base, ff_tile=ff_tile, final=final),
        grid=(b, s // tm),
        in_specs=in_specs,
        out_specs=pl.BlockSpec((1, tm, d), lambda i, j: (i, j, 0)),
        out_shape=jax.ShapeDtypeStruct((b, s, d), F32),
        compiler_params=_cparams(2),
        name="ffn_final" if final else "ffn",
    )(*args)


def _qkv_kernel(x_ref, mod_ref, g_ref, w_ref, o_ref, *, mod_base, row_chunk, n_scaled):
    x = x_ref[0]
    shift = mod_ref[0, mod_base:mod_base + 1, :]
    scale = mod_ref[0, mod_base + 1:mod_base + 2, :]
    hb = _modulate(x, g_ref[...], shift, scale).astype(BF16)
    q_scale = HEAD_DIM ** -0.5
    for r in range(w_ref.shape[0] // row_chunk):
        rows = slice(r * row_chunk, (r + 1) * row_chunk)
        res = lax.dot_general(w_ref[rows, :], hb, (((1,), (1,)), ((), ())), preferred_element_type=F32)
        if r in n_scaled:
            res = res * q_scale
        o_ref[0, 0, rows, :] = res.astype(BF16)


def _qkv(x, mod, g, w_in_t, mod_base, q_chunks):
    b, s, d = x.shape
    n = w_in_t.shape[0]
    t = ATT_TILE
    row_chunk = 512
    return pl.pallas_call(
        functools.partial(_qkv_kernel, mod_base=mod_base, row_chunk=row_chunk, n_scaled=q_chunks),
        grid=(b, s // t),
        in_specs=[
            pl.BlockSpec((1, t, d), lambda i, j: (i, j, 0)),
            pl.BlockSpec((1, N_MOD, d), lambda i, j: (i, 0, 0)),
            _resident((1, d), lambda i, j: (0, 0)),
            _resident((n, d), lambda i, j: (0, 0)),
        ],
        out_specs=pl.BlockSpec((1, 1, n, t), lambda i, j: (i, j, 0, 0)),
        out_shape=jax.ShapeDtypeStruct((b, s // t, n, t), BF16),
        compiler_params=_cparams(2),
        name="qkv_proj",
    )(x, mod, g.reshape(1, d), w_in_t)


def _head_rmsnorm_t(o, g_col):
    return (o * lax.rsqrt(jnp.mean(o * o, axis=0, keepdims=True) + EPS)) * g_col


_CONTRACT_ROWS = (((0,), (0,)), ((), ()))


def _sb_kernel(q_ref, k_ref, v_ref, u_ref, g_ref, o_ref):
    nt, t = q_ref.shape[1], q_ref.shape[3]
    row = lax.broadcasted_iota(jnp.int32, (t, t), 0)
    col = lax.broadcasted_iota(jnp.int32, (t, t), 1)
    causal = row < col
    ucat = u_ref[...]
    g_col = g_ref[0]

    def tile(q_t, kj, carry, diag):
        k_t = k_ref[0, kj]
        v_t = v_ref[0, kj]
        z = lax.dot_general(k_t, q_t, _CONTRACT_ROWS, preferred_element_type=F32)
        na = -jnp.abs(z)
        sp = jnp.log(1.0 + jnp.exp(na))
        mz = jnp.minimum(z, 0.0)
        ls = mz - sp
        ln = (na - mz) - sp
        if diag:
            ln = jnp.where(causal, ln, 0.0)
        hi = ln.astype(BF16)
        lo = (ln - hi.astype(F32)).astype(BF16)
        suffix = jnp.dot(ucat, jnp.concatenate([hi, lo], axis=0), preferred_element_type=F32)
        w = jnp.exp((ls + suffix) + carry)
        if diag:
            w = jnp.where(causal, w, 0.0)
        pv = jnp.dot(v_t, w.astype(BF16), preferred_element_type=F32)
        total = suffix[0:1, :] + ln[0:1, :]
        return pv, carry + total

    def q_block(qi, _):
        q_t = q_ref[0, qi]
        acc, carry = tile(q_t, qi, jnp.zeros((1, t), F32), True)

        def k_block(i, st):
            acc, carry = st
            pv, carry = tile(q_t, qi - 1 - i, carry, False)
            return acc + pv, carry

        acc, carry = lax.fori_loop(0, qi, k_block, (acc, carry))
        o_ref[0, qi] = _head_rmsnorm_t(acc, g_col).astype(BF16)
        return 0

    lax.fori_loop(0, nt, q_block, 0)


def _suffix_matrix(t):
    u = (np.arange(t)[None, :] > np.arange(t)[:, None]).astype(np.float32)
    return jnp.asarray(np.concatenate([u, u], axis=1), dtype=BF16)


def _sb_attention(qkv_t, g_out, q_blk, k_blk, v_blk):
    b, nt, _, t = qkv_t.shape
    nh = g_out.shape[0]
    blk = (1, nt, HEAD_DIM, t)
    return pl.pallas_call(
        _sb_kernel,
        grid=(b, nh),
        in_specs=[
            pl.BlockSpec(blk, lambda i, h: (i, 0, q_blk + h, 0)),
            pl.BlockSpec(blk, lambda i, h: (i, 0, k_blk + h, 0)),
            pl.BlockSpec(blk, lambda i, h: (i, 0, v_blk + h, 0)),
            _resident((t, 2 * t), lambda i, h: (0, 0)),
            pl.BlockSpec((1, HEAD_DIM, 1), lambda i, h: (h, 0, 0)),
        ],
        out_specs=pl.BlockSpec(blk, lambda i, h: (i, 0, h, 0)),
        out_shape=jax.ShapeDtypeStruct((b, nt, nh * HEAD_DIM, t), BF16),
        compiler_params=_cparams(2),
        name="stickbreak_attn",
    )(qkv_t, qkv_t, qkv_t, _suffix_matrix(t), g_out.reshape(nh, HEAD_DIM, 1))


def _t5_causal_bucket(n, n_buckets):
    max_exact = n_buckets // 2
    nf = np.maximum(n, 1).astype(np.float32)
    large = max_exact + (np.log(nf / max_exact) / math.log(MAX_DISTANCE / max_exact)
                         * (n_buckets - max_exact)).astype(np.int32)
    large = np.minimum(large, n_buckets - 1)
    return np.where(n < max_exact, n, large).astype(np.int32)


def _dilated_bias_tiles(rel_bias, seq, t):
    dist = np.arange(seq)
    mult = np.zeros(seq, np.float32)
    for window, dilation in DIL_CONFIGS:
        mult += ((dist % dilation == 0) & (dist <= window)).astype(np.float32)
    bucket = _t5_causal_bucket(dist, rel_bias.shape[0])
    per_dist = rel_bias[bucket].astype(F32) + jnp.log(jnp.maximum(jnp.asarray(mult), 1.0))[:, None]
    per_dist = jnp.where(jnp.asarray(mult > 0)[:, None], per_dist, NEG_INF)
    o = np.arange(seq // t)[:, None, None]
    r = np.arange(t)[None, :, None]
    c = np.arange(t)[None, None, :]
    dd = o * t + c - r
    tiles = jnp.where(jnp.asarray(dd >= 0)[..., None], per_dist[np.clip(dd, 0, seq - 1)], NEG_INF)
    return jnp.transpose(tiles, (3, 0, 1, 2))


def _dil_kernel(q_ref, k_ref, v_ref, bias_ref, g_ref, o_ref):
    nt = q_ref.shape[1]
    g_col = g_ref[0]

    def scores(q_t, qi, kj):
        z = lax.dot_general(k_ref[0, kj], q_t, _CONTRACT_ROWS, preferred_element_type=F32)
        return z + bias_ref[0, qi - kj]

    def q_block(qi, _):
        q_t = q_ref[0, qi]
        z = scores(q_t, qi, qi)
        m = jnp.max(z, axis=0, keepdims=True)
        p = jnp.exp(z - m)
        l = jnp.sum(p, axis=0, keepdims=True)
        acc = jnp.dot(v_ref[0, qi], p.astype(BF16), preferred_element_type=F32)

        def k_block(i, st):
            m, l, acc = st
            kj = qi - 1 - i
            z = scores(q_t, qi, kj)
            m_new = jnp.maximum(m, jnp.max(z, axis=0, keepdims=True))
            alpha = jnp.exp(m - m_new)
            p = jnp.exp(z - m_new)
            l = alpha * l + jnp.sum(p, axis=0, keepdims=True)
            acc = alpha * acc + jnp.dot(v_ref[0, kj], p.astype(BF16), preferred_element_type=F32)
            return m_new, l, acc

        m, l, acc = lax.fori_loop(0, qi, k_block, (m, l, acc))
        o_ref[0, qi] = _head_rmsnorm_t(acc / l, g_col).astype(BF16)
        return 0

    lax.fori_loop(0, nt, q_block, 0)


def _dil_attention(qkv_t, bias_tiles, g_out, q_blk, k_blk, v_blk):
    b, nt, _, t = qkv_t.shape
    nh = g_out.shape[0]
    blk = (1, nt, HEAD_DIM, t)
    return pl.pallas_call(
        _dil_kernel,
        grid=(nh, b),
        in_specs=[
            pl.BlockSpec(blk, lambda h, i: (i, 0, q_blk + h, 0)),
            pl.BlockSpec(blk, lambda h, i: (i, 0, k_blk + h, 0)),
            pl.BlockSpec(blk, lambda h, i: (i, 0, v_blk + h, 0)),
            pl.BlockSpec((1, nt, t, t), lambda h, i: (h, 0, 0, 0)),
            pl.BlockSpec((1, HEAD_DIM, 1), lambda h, i: (h, 0, 0)),
        ],
        out_specs=pl.BlockSpec(blk, lambda h, i: (i, 0, h, 0)),
        out_shape=jax.ShapeDtypeStruct((b, nt, nh * HEAD_DIM, t), BF16),
        compiler_params=_cparams(2),
        name="dilated_attn",
    )(qkv_t, qkv_t, qkv_t, bias_tiles, g_out.reshape(nh, HEAD_DIM, 1))


def _out_kernel(x_ref, mod_ref, osb_ref, odil_ref, wsb_ref, wdil_ref, o_ref, *, mod_base):
    gate = mod_ref[0, mod_base + 2:mod_base + 3, :]
    y = lax.dot_general(osb_ref[0, 0], wsb_ref[...], _CONTRACT_ROWS, preferred_element_type=F32)
    y = y + lax.dot_general(odil_ref[0, 0], wdil_ref[...], _CONTRACT_ROWS, preferred_element_type=F32)
    o_ref[0] = x_ref[0] + gate * y


def _out_proj(x, mod, o_sb, o_dil, w_sb, w_dil, mod_base):
    b, s, d = x.shape
    _, nt, dsb, t = o_sb.shape
    ddil = o_dil.shape[2]
    return pl.pallas_call(
        functools.partial(_out_kernel, mod_base=mod_base),
        grid=(b, nt),
        in_specs=[
            pl.BlockSpec((1, t, d), lambda i, j: (i, j, 0)),
            pl.BlockSpec((1, N_MOD, d), lambda i, j: (i, 0, 0)),
            pl.BlockSpec((1, 1, dsb, t), lambda i, j: (i, j, 0, 0)),
            pl.BlockSpec((1, 1, ddil, t), lambda i, j: (i, j, 0, 0)),
            _resident((dsb, d), lambda i, j: (0, 0)),
            _resident((ddil, d), lambda i, j: (0, 0)),
        ],
        out_specs=pl.BlockSpec((1, t, d), lambda i, j: (i, j, 0)),
        out_shape=jax.ShapeDtypeStruct((b, s, d), F32),
        compiler_params=_cparams(2),
        name="out_proj",
    )(x, mod, o_sb, o_dil, w_sb, w_dil)


def kernel(x, c, w_ada, b_ada, g_ffn1, w1_gate, w1_up, w1_down, g_mix, w_in, g_sb_out, g_dil_out, w_out, rel_bias, g_ffn2, w2_gate, w2_up, w2_down, g_final):
    depth = w_ada.shape[0]
    seq = x.shape[1]
    nh_sb = g_sb_out.shape[1]
    nh_dil = g_dil_out.shape[1]
    d_sb = nh_sb * HEAD_DIM
    sb_blk = (0, nh_sb, 2 * nh_sb)
    dil_blk = (3 * nh_sb, 3 * nh_sb + nh_dil, 3 * nh_sb + 2 * nh_dil)
    q_chunks = (0, 3 * d_sb // 512)
    bias_tiles = _dilated_bias_tiles(rel_bias, seq, ATT_TILE)
    for l in range(depth):
        mod = _adaln_mod(c, w_ada[l], b_ada[l])
        x = _ffn(x, mod, g_ffn1[l], w1_gate[l].astype(BF16), w1_up[l].astype(BF16), w1_down[l].astype(BF16), 0)
        qkv_t = _qkv(x, mod, g_mix[l], w_in[l].T.astype(BF16), 3, q_chunks)
        o_sb = _sb_attention(qkv_t, g_sb_out[l], *sb_blk)
        o_dil = _dil_attention(qkv_t, bias_tiles, g_dil_out[l], *dil_blk)
        w_o = w_out[l].astype(BF16)
        x = _out_proj(x, mod, o_sb, o_dil, w_o[:d_sb], w_o[d_sb:], 3)
        last = l == depth - 1
        x = _ffn(x, mod, g_ffn2[l], w2_gate[l].astype(BF16), w2_up[l].astype(BF16), w2_down[l].astype(BF16), 6,
                 g_final if last else None)
    return x
```

```python
import functools
import math

import jax
import jax.numpy as jnp
import numpy as np
from jax import lax
from jax.experimental import pallas as pl
from jax.experimental.pallas import tpu as pltpu

F32 = jnp.float32
BF16 = jnp.bfloat16

HEAD_DIM = 64
DIL_CONFIGS = ((128, 1), (512, 4), (2048, 16))
MAX_DISTANCE = 2048
N_MOD = 9
EPS = 1e-6
NEG_INF = -1e30

ATT_TILE = 256
SB_SUB_TILE = 128
LOG2E = 1.4426950408889634
FFN_TOKEN_TILE = 512
VMEM_LIMIT = 56 * 1024 * 1024


def _cparams(n_grid):
    return pltpu.CompilerParams(dimension_semantics=("arbitrary",) * n_grid, vmem_limit_bytes=VMEM_LIMIT)


def _resident(block_shape, index_map):
    return pl.BlockSpec(block_shape, index_map, pipeline_mode=pl.Buffered(1))


def _mod_kernel(c_ref, w_ref, b_ref, o_ref):
    c = c_ref[...]
    s = c * jax.nn.sigmoid(c)
    o_ref[...] = jnp.dot(s.astype(BF16), w_ref[...].astype(BF16), preferred_element_type=F32) + b_ref[...]


def _adaln_mod(c, w_ada, b_ada):
    b, d = c.shape
    n = w_ada.shape[1]
    tn = 1024
    out = pl.pallas_call(
        _mod_kernel,
        grid=(n // tn,),
        in_specs=[
            pl.BlockSpec((b, d), lambda j: (0, 0)),
            pl.BlockSpec((d, tn), lambda j: (0, j)),
            pl.BlockSpec((1, tn), lambda j: (0, j)),
        ],
        out_specs=pl.BlockSpec((b, tn), lambda j: (0, j)),
        out_shape=jax.ShapeDtypeStruct((b, n), F32),
        compiler_params=_cparams(1),
        name="adaln_mod",
    )(c, w_ada, b_ada.reshape(1, n))
    return out.reshape(b, N_MOD, d)


def _modulate(x, g, shift, scale):
    y = x * lax.rsqrt(jnp.mean(x * x, axis=-1, keepdims=True) + EPS)
    return (y * g) * (1.0 + scale) + shift


def _ffn_kernel(*refs, mod_base, ff_tile, final):
    if final:
        x_ref, mod_ref, g_ref, wg_ref, wu_ref, wd_ref, gf_ref, o_ref = refs
    else:
        x_ref, mod_ref, g_ref, wg_ref, wu_ref, wd_ref, o_ref = refs
    x = x_ref[0]
    shift = mod_ref[0, mod_base:mod_base + 1, :]
    scale = mod_ref[0, mod_base + 1:mod_base + 2, :]
    gate = mod_ref[0, mod_base + 2:mod_base + 3, :]
    hb = _modulate(x, g_ref[...], shift, scale).astype(BF16)
    acc = None
    for j in range(wg_ref.shape[1] // ff_tile):
        sl = slice(j * ff_tile, (j + 1) * ff_tile)
        gg = jnp.dot(hb, wg_ref[:, sl], preferred_element_type=F32)
        uu = jnp.dot(hb, wu_ref[:, sl], preferred_element_type=F32)
        a = ((gg * jax.nn.sigmoid(gg)) * uu).astype(BF16)
        dd = jnp.dot(a, wd_ref[sl, :], preferred_element_type=F32)
        acc = dd if acc is None else acc + dd
    out = x + (0.5 * gate) * acc
    if final:
        out = out * lax.rsqrt(jnp.mean(out * out, axis=-1, keepdims=True) + EPS) * gf_ref[...]
    o_ref[0] = out


def _ffn(x, mod, g, wg, wu, wd, mod_base, g_final=None):
    b, s, d = x.shape
    dff = wg.shape[1]
    tm = FFN_TOKEN_TILE
    ff_tile = dff // 2
    final = g_final is not None
    in_specs = [
        pl.BlockSpec((1, tm, d), lambda i, j: (i, j, 0)),
        pl.BlockSpec((1, N_MOD, d), lambda i, j: (i, 0, 0)),
        _resident((1, d), lambda i, j: (0, 0)),
        _resident((d, dff), lambda i, j: (0, 0)),
        _resident((d, dff), lambda i, j: (0, 0)),
        _resident((dff, d), lambda i, j: (0, 0)),
    ]
    args = [x, mod, g.reshape(1, d), wg, wu, wd]
    if final:
        in_specs.append(_resident((1, d), lambda i, j: (0, 0)))
        args.append(g_final.reshape(1, d))
    return pl.pallas_call(
        functools.partial(_ffn_kernel, mod_base=mod_base, ff_tile=ff_tile, final=final),
        grid=(b, s // tm),
        in_specs=in_specs,
        out_specs=pl.BlockSpec((1, tm, d), lambda i, j: (i, j, 0)),
        out_shape=jax.ShapeDtypeStruct((b, s, d), F32),
        compiler_params=_cparams(2),
        name="ffn_final" if final else "ffn",
    )(*args)


def _qkv_kernel(x_ref, mod_ref, g_ref, w_ref, o_ref, *, mod_base, row_chunk, n_scaled):
    x = x_ref[0]
    shift = mod_ref[0, mod_base:mod_base + 1, :]
    scale = mod_ref[0, mod_base + 1:mod_base + 2, :]
    hb = _modulate(x, g_ref[...], shift, scale).astype(BF16)
    q_scale = HEAD_DIM ** -0.5 * LOG2E
    for r in range(w_ref.shape[0] // row_chunk):
        rows = slice(r * row_chunk, (r + 1) * row_chunk)
        res = lax.dot_general(w_ref[rows, :], hb, (((1,), (1,)), ((), ())), preferred_element_type=F32)
        if r in n_scaled:
            res = res * q_scale
        o_ref[0, 0, rows, :] = res.astype(BF16)


def _qkv(x, mod, g, w_in_t, mod_base, q_chunks):
    b, s, d = x.shape
    n = w_in_t.shape[0]
    t = ATT_TILE
    row_chunk = 512
    return pl.pallas_call(
        functools.partial(_qkv_kernel, mod_base=mod_base, row_chunk=row_chunk, n_scaled=q_chunks),
        grid=(b, s // t),
        in_specs=[
            pl.BlockSpec((1, t, d), lambda i, j: (i, j, 0)),
            pl.BlockSpec((1, N_MOD, d), lambda i, j: (i, 0, 0)),
            _resident((1, d), lambda i, j: (0, 0)),
            _resident((n, d), lambda i, j: (0, 0)),
        ],
        out_specs=pl.BlockSpec((1, 1, n, t), lambda i, j: (i, j, 0, 0)),
        out_shape=jax.ShapeDtypeStruct((b, s // t, n, t), BF16),
        compiler_params=_cparams(2),
        name="qkv_proj",
    )(x, mod, g.reshape(1, d), w_in_t)


def _head_rmsnorm_t(o, g_col):
    return (o * lax.rsqrt(jnp.mean(o * o, axis=0, keepdims=True) + EPS)) * g_col


_CONTRACT_ROWS = (((0,), (0,)), ((), ()))


def _sb_kernel(q_ref, k_ref, v_ref, u_ref, g_ref, o_ref):
    nt, t = q_ref.shape[1], q_ref.shape[3]
    ks = u_ref.shape[0]
    sub = t // ks
    row = lax.broadcasted_iota(jnp.int32, (ks, t), 0)
    col = lax.broadcasted_iota(jnp.int32, (ks, t), 1)
    ucat = u_ref[...]
    g_col = g_ref[0]
    tiles = [(qi, kj) for qi in range(nt) for kj in range(qi, -1, -1)]
    subs = range(sub - 1, -1, -1)
    z_of, cat_of, incl_of, w_of = {}, {}, {}, {}
    state = {}

    def valid_mask(si):
        return row + (si * ks) < col

    def scores(n):
        qi, kj = tiles[n]
        q_t = q_ref[0, qi]
        z_of[n] = [lax.dot_general(k_ref[0, kj, :, si * ks:(si + 1) * ks], q_t, _CONTRACT_ROWS,
                                   preferred_element_type=F32) for si in range(sub)]

    def softplus(n):
        qi, kj = tiles[n]
        cats = []
        for si in range(sub):
            z = z_of[n][si]
            sp = jnp.maximum(z, 0.0) + jnp.log(1.0 + jnp.exp2(-jnp.abs(z))) * LOG2E
            if kj == qi:
                sp = jnp.where(valid_mask(si), sp, 0.0)
            hi = sp.astype(BF16)
            lo = (sp - hi.astype(F32)).astype(BF16)
            cats.append(jnp.concatenate([hi, lo], axis=0))
        cat_of[n] = cats

    def cumsum(n):
        incl_of[n] = [jnp.dot(ucat, c, preferred_element_type=F32) for c in cat_of.pop(n)]

    def weights(n):
        qi, kj = tiles[n]
        carry = state.setdefault(qi, [None, None])[0]
        zs, incls = z_of.pop(n), incl_of.pop(n)
        parts = [None] * sub
        for si in subs:
            arg = zs[si] - incls[si]
            if carry is not None:
                arg = arg - carry
            w = jnp.exp2(arg)
            if kj == qi:
                w = jnp.where(valid_mask(si), w, 0.0)
            parts[si] = w.astype(BF16)
            total = incls[si][0:1, :]
            carry = total if carry is None else carry + total
        state[qi][0] = carry
        w_of[n] = jnp.concatenate(parts, axis=0)

    def values(n):
        qi, kj = tiles[n]
        pv = jnp.dot(v_ref[0, kj], w_of.pop(n), preferred_element_type=F32)
        acc = state[qi][1]
        acc = pv if acc is None else acc + pv
        state[qi][1] = acc
        if kj == 0:
            o_ref[0, qi] = _head_rmsnorm_t(acc, g_col).astype(BF16)
            del state[qi]

    n_tiles = len(tiles)
    for step in range(n_tiles + 2):
        if step < n_tiles:
            scores(step)
        if 0 <= step - 1 < n_tiles:
            softplus(step - 1)
            cumsum(step - 1)
        if 0 <= step - 2 < n_tiles:
            weights(step - 2)
            values(step - 2)


def _suffix_matrix(n):
    u = (np.arange(n)[None, :] >= np.arange(n)[:, None]).astype(np.float32)
    return jnp.asarray(np.concatenate([u, u], axis=1), dtype=BF16)


def _sb_attention(qkv_t, g_out, q_blk, k_blk, v_blk):
    b, nt, _, t = qkv_t.shape
    nh = g_out.shape[0]
    blk = (1, nt, HEAD_DIM, t)
    return pl.pallas_call(
        _sb_kernel,
        grid=(b, nh),
        in_specs=[
            pl.BlockSpec(blk, lambda i, h: (i, 0, q_blk + h, 0)),
            pl.BlockSpec(blk, lambda i, h: (i, 0, k_blk + h, 0)),
            pl.BlockSpec(blk, lambda i, h: (i, 0, v_blk + h, 0)),
            _resident((SB_SUB_TILE, 2 * SB_SUB_TILE), lambda i, h: (0, 0)),
            pl.BlockSpec((1, HEAD_DIM, 1), lambda i, h: (h, 0, 0)),
        ],
        out_specs=pl.BlockSpec(blk, lambda i, h: (i, 0, h, 0)),
        out_shape=jax.ShapeDtypeStruct((b, nt, nh * HEAD_DIM, t), BF16),
        compiler_params=_cparams(2),
        name="stickbreak_attn",
    )(qkv_t, qkv_t, qkv_t, _suffix_matrix(SB_SUB_TILE), g_out.reshape(nh, HEAD_DIM, 1))


def _t5_causal_bucket(n, n_buckets):
    max_exact = n_buckets // 2
    nf = np.maximum(n, 1).astype(np.float32)
    large = max_exact + (np.log(nf / max_exact) / math.log(MAX_DISTANCE / max_exact)
                         * (n_buckets - max_exact)).astype(np.int32)
    large = np.minimum(large, n_buckets - 1)
    return np.where(n < max_exact, n, large).astype(np.int32)


def _dilated_bias_tiles(rel_bias, seq, t):
    dist = np.arange(seq)
    mult = np.zeros(seq, np.float32)
    for window, dilation in DIL_CONFIGS:
        mult += ((dist % dilation == 0) & (dist <= window)).astype(np.float32)
    bucket = _t5_causal_bucket(dist, rel_bias.shape[0])
    per_dist = (rel_bias[bucket].astype(F32) + jnp.log(jnp.maximum(jnp.asarray(mult), 1.0))[:, None]) * LOG2E
    per_dist = jnp.where(jnp.asarray(mult > 0)[:, None], per_dist, NEG_INF)
    nh = rel_bias.shape[1]
    table = jnp.concatenate([per_dist.T, jnp.full((nh, seq), NEG_INF, F32)], axis=1)
    ln = 2 * seq
    skewed = jnp.broadcast_to(table[:, None, :], (nh, t, ln)).reshape(nh, t * ln)[:, :t * (ln - 1)]
    skewed = skewed.reshape(nh, t, ln - 1)[:, :, :seq]
    return jnp.transpose(skewed.reshape(nh, t, seq // t, t), (0, 2, 1, 3))


def _dil_kernel(q_ref, k_ref, v_ref, bias_ref, g_ref, o_ref):
    nt = q_ref.shape[1]
    g_col = g_ref[0]
    z_of = {}

    def scores(qi):
        q_t = q_ref[0, qi]
        zs = []
        m = None
        for kj in range(qi + 1):
            z = lax.dot_general(k_ref[0, kj], q_t, _CONTRACT_ROWS, preferred_element_type=F32)
            z = z + bias_ref[0, qi - kj]
            zs.append(z)
            zm = jnp.max(z, axis=0, keepdims=True)
            m = zm if m is None else jnp.maximum(m, zm)
        z_of[qi] = (zs, m)

    def attend(qi):
        zs, m = z_of.pop(qi)
        l = None
        acc = None
        for kj in range(qi + 1):
            p = jnp.exp2(zs[kj] - m)
            ps = jnp.sum(p, axis=0, keepdims=True)
            pv = jnp.dot(v_ref[0, kj], p.astype(BF16), preferred_element_type=F32)
            l = ps if l is None else l + ps
            acc = pv if acc is None else acc + pv
        o_ref[0, qi] = _head_rmsnorm_t(acc / l, g_col).astype(BF16)

    scores(0)
    for qi in range(nt):
        if qi + 1 < nt:
            scores(qi + 1)
        attend(qi)


def _dil_attention(qkv_t, bias_tiles, g_out, q_blk, k_blk, v_blk):
    b, nt, _, t = qkv_t.shape
    nh = g_out.shape[0]
    blk = (1, nt, HEAD_DIM, t)
    return pl.pallas_call(
        _dil_kernel,
        grid=(nh, b),
        in_specs=[
            pl.BlockSpec(blk, lambda h, i: (i, 0, q_blk + h, 0)),
            pl.BlockSpec(blk, lambda h, i: (i, 0, k_blk + h, 0)),
            pl.BlockSpec(blk, lambda h, i: (i, 0, v_blk + h, 0)),
            pl.BlockSpec((1, nt, t, t), lambda h, i: (h, 0, 0, 0)),
            pl.BlockSpec((1, HEAD_DIM, 1), lambda h, i: (h, 0, 0)),
        ],
        out_specs=pl.BlockSpec(blk, lambda h, i: (i, 0, h, 0)),
        out_shape=jax.ShapeDtypeStruct((b, nt, nh * HEAD_DIM, t), BF16),
        compiler_params=_cparams(2),
        name="dilated_attn",
    )(qkv_t, qkv_t, qkv_t, bias_tiles, g_out.reshape(nh, HEAD_DIM, 1))


def _out_kernel(x_ref, mod_ref, osb_ref, odil_ref, wsb_ref, wdil_ref, o_ref, *, mod_base):
    gate = mod_ref[0, mod_base + 2:mod_base + 3, :]
    y = lax.dot_general(osb_ref[0, 0], wsb_ref[...], _CONTRACT_ROWS, preferred_element_type=F32)
    y = y + lax.dot_general(odil_ref[0, 0], wdil_ref[...], _CONTRACT_ROWS, preferred_element_type=F32)
    o_ref[0] = x_ref[0] + gate * y


def _out_proj(x, mod, o_sb, o_dil, w_sb, w_dil, mod_base):
    b, s, d = x.shape
    _, nt, dsb, t = o_sb.shape
    ddil = o_dil.shape[2]
    return pl.pallas_call(
        functools.partial(_out_kernel, mod_base=mod_base),
        grid=(b, nt),
        in_specs=[
            pl.BlockSpec((1, t, d), lambda i, j: (i, j, 0)),
            pl.BlockSpec((1, N_MOD, d), lambda i, j: (i, 0, 0)),
            pl.BlockSpec((1, 1, dsb, t), lambda i, j: (i, j, 0, 0)),
            pl.BlockSpec((1, 1, ddil, t), lambda i, j: (i, j, 0, 0)),
            _resident((dsb, d), lambda i, j: (0, 0)),
            _resident((ddil, d), lambda i, j: (0, 0)),
        ],
        out_specs=pl.BlockSpec((1, t, d), lambda i, j: (i, j, 0)),
        out_shape=jax.ShapeDtypeStruct((b, s, d), F32),
        compiler_params=_cparams(2),
        name="out_proj",
    )(x, mod, o_sb, o_dil, w_sb, w_dil)


def kernel(x, c, w_ada, b_ada, g_ffn1, w1_gate, w1_up, w1_down, g_mix, w_in, g_sb_out, g_dil_out, w_out, rel_bias, g_ffn2, w2_gate, w2_up, w2_down, g_final):
    depth = w_ada.shape[0]
    seq = x.shape[1]
    nh_sb = g_sb_out.shape[1]
    nh_dil = g_dil_out.shape[1]
    d_sb = nh_sb * HEAD_DIM
    sb_blk = (0, nh_sb, 2 * nh_sb)
    dil_blk = (3 * nh_sb, 3 * nh_sb + nh_dil, 3 * nh_sb + 2 * nh_dil)
    q_chunks = (0, 3 * d_sb // 512)
    bias_tiles = _dilated_bias_tiles(rel_bias, seq, ATT_TILE)
    for l in range(depth):
        mod = _adaln_mod(c, w_ada[l], b_ada[l])
        x = _ffn(x, mod, g_ffn1[l], w1_gate[l].astype(BF16), w1_up[l].astype(BF16), w1_down[l].astype(BF16), 0)
        qkv_t = _qkv(x, mod, g_mix[l], w_in[l].T.astype(BF16), 3, q_chunks)
        o_sb = _sb_attention(qkv_t, g_sb_out[l], *sb_blk)
        o_dil = _dil_attention(qkv_t, bias_tiles, g_dil_out[l], *dil_blk)
        w_o = w_out[l].astype(BF16)
        x = _out_proj(x, mod, o_sb, o_dil, w_o[:d_sb], w_o[d_sb:], 3)
        last = l == depth - 1
        x = _ffn(x, mod, g_ffn2[l], w2_gate[l].astype(BF16), w2_up[l].astype(BF16), w2_down[l].astype(BF16), 6,
                 g_final if last else None)
    return x
```

```python
import functools
import math

import jax
import jax.numpy as jnp
import numpy as np
from jax import lax
from jax.experimental import pallas as pl
from jax.experimental.pallas import tpu as pltpu

F32 = jnp.float32
BF16 = jnp.bfloat16

HEAD_DIM = 64
DIL_CONFIGS = ((128, 1), (512, 4), (2048, 16))
MAX_DISTANCE = 2048
N_MOD = 9
EPS = 1e-6
NEG_INF = -1e30

ATT_TILE = 256
SB_SUB_TILE = 128
LOG2E = 1.4426950408889634
FFN_TOKEN_TILE = 512
VMEM_LIMIT = 56 * 1024 * 1024


def _cparams(n_grid):
    return pltpu.CompilerParams(dimension_semantics=("arbitrary",) * n_grid, vmem_limit_bytes=VMEM_LIMIT)


def _resident(block_shape, index_map):
    return pl.BlockSpec(block_shape, index_map, pipeline_mode=pl.Buffered(1))


def _mod_kernel(c_ref, w_ref, b_ref, o_ref):
    c = c_ref[...]
    s = c * jax.nn.sigmoid(c)
    o_ref[...] = jnp.dot(s.astype(BF16), w_ref[...].astype(BF16), preferred_element_type=F32) + b_ref[...]


def _adaln_mod(c, w_ada, b_ada):
    b, d = c.shape
    n = w_ada.shape[1]
    tn = 1024
    out = pl.pallas_call(
        _mod_kernel,
        grid=(n // tn,),
        in_specs=[
            pl.BlockSpec((b, d), lambda j: (0, 0)),
            pl.BlockSpec((d, tn), lambda j: (0, j)),
            pl.BlockSpec((1, tn), lambda j: (0, j)),
        ],
        out_specs=pl.BlockSpec((b, tn), lambda j: (0, j)),
        out_shape=jax.ShapeDtypeStruct((b, n), F32),
        compiler_params=_cparams(1),
        name="adaln_mod",
    )(c, w_ada, b_ada.reshape(1, n))
    return out.reshape(b, N_MOD, d)


def _modulate(x, g, shift, scale):
    y = x * lax.rsqrt(jnp.mean(x * x, axis=-1, keepdims=True) + EPS)
    return (y * g) * (1.0 + scale) + shift


def _ffn_kernel(*refs, mod_base, mixer_base, final):
    x_ref, mod_ref, g_ref, wg_ref, wu_ref, wd_ref = refs[:6]
    rest = list(refs[6:])
    o_ref = rest.pop()
    gf_ref = rest.pop() if final else None
    mixer_refs = rest if mixer_base is not None else None
    shift = mod_ref[0, mod_base:mod_base + 1, :]
    scale = mod_ref[0, mod_base + 1:mod_base + 2, :]
    gate = mod_ref[0, mod_base + 2:mod_base + 3, :]
    sub = ATT_TILE
    n_sub = x_ref.shape[1] // sub
    xs, hs, acts = {}, {}, {}

    def prologue(i):
        x = x_ref[0, i * sub:(i + 1) * sub, :]
        if mixer_refs is not None:
            osb_ref, odil_ref, wsb_ref, wdil_ref = mixer_refs
            y = lax.dot_general(osb_ref[0, i], wsb_ref[...], _CONTRACT_ROWS, preferred_element_type=F32)
            y = y + lax.dot_general(odil_ref[0, i], wdil_ref[...], _CONTRACT_ROWS, preferred_element_type=F32)
            x = x + mod_ref[0, mixer_base + 2:mixer_base + 3, :] * y
        xs[i] = x
        hs[i] = _modulate(x, g_ref[...], shift, scale).astype(BF16)

    def expand(i):
        hb = hs.pop(i)
        gg = jnp.dot(hb, wg_ref[...], preferred_element_type=F32)
        uu = jnp.dot(hb, wu_ref[...], preferred_element_type=F32)
        acts[i] = ((gg * jax.nn.sigmoid(gg)) * uu).astype(BF16)

    def contract(i):
        out = xs.pop(i) + (0.5 * gate) * jnp.dot(acts.pop(i), wd_ref[...], preferred_element_type=F32)
        if final:
            out = out * lax.rsqrt(jnp.mean(out * out, axis=-1, keepdims=True) + EPS) * gf_ref[...]
        o_ref[0, i * sub:(i + 1) * sub, :] = out

    for i in range(n_sub):
        prologue(i)
    for i in range(n_sub):
        expand(i)
    for i in range(n_sub):
        contract(i)


def _ffn(x, mod, g, wg, wu, wd, mod_base, mixer=None, g_final=None):
    b, s, d = x.shape
    dff = wg.shape[1]
    tm = FFN_TOKEN_TILE
    final = g_final is not None
    in_specs = [
        pl.BlockSpec((1, tm, d), lambda i, j: (i, j, 0)),
        pl.BlockSpec((1, N_MOD, d), lambda i, j: (i, 0, 0)),
        _resident((1, d), lambda i, j: (0, 0)),
        _resident((d, dff), lambda i, j: (0, 0)),
        _resident((d, dff), lambda i, j: (0, 0)),
        _resident((dff, d), lambda i, j: (0, 0)),
    ]
    args = [x, mod, g.reshape(1, d), wg, wu, wd]
    mixer_base = None
    if mixer is not None:
        o_sb, o_dil, w_sb, w_dil, mixer_base = mixer
        t = o_sb.shape[3]
        for o_heads in (o_sb, o_dil):
            in_specs.append(pl.BlockSpec((1, tm // t, o_heads.shape[2], t), lambda i, j: (i, j, 0, 0)))
            args.append(o_heads)
        for w_heads in (w_sb, w_dil):
            in_specs.append(_resident(w_heads.shape, lambda i, j: (0, 0)))
            args.append(w_heads)
    if final:
        in_specs.append(_resident((1, d), lambda i, j: (0, 0)))
        args.append(g_final.reshape(1, d))
    return pl.pallas_call(
        functools.partial(_ffn_kernel, mod_base=mod_base, mixer_base=mixer_base, final=final),
        grid=(b, s // tm),
        in_specs=in_specs,
        out_specs=pl.BlockSpec((1, tm, d), lambda i, j: (i, j, 0)),
        out_shape=jax.ShapeDtypeStruct((b, s, d), F32),
        compiler_params=_cparams(2),
        name="ffn_mixer_out" if mixer is not None else "ffn",
    )(*args)


def _qkv_kernel(x_ref, mod_ref, g_ref, w_ref, o_ref, *, mod_base, row_chunk, n_scaled):
    x = x_ref[0]
    shift = mod_ref[0, mod_base:mod_base + 1, :]
    scale = mod_ref[0, mod_base + 1:mod_base + 2, :]
    hb = _modulate(x, g_ref[...], shift, scale).astype(BF16)
    q_scale = HEAD_DIM ** -0.5 * LOG2E
    for r in range(w_ref.shape[0] // row_chunk):
        rows = slice(r * row_chunk, (r + 1) * row_chunk)
        res = lax.dot_general(w_ref[rows, :], hb, (((1,), (1,)), ((), ())), preferred_element_type=F32)
        if r in n_scaled:
            res = res * q_scale
        o_ref[0, 0, rows, :] = res.astype(BF16)


def _qkv(x, mod, g, w_in_t, mod_base, q_chunks):
    b, s, d = x.shape
    n = w_in_t.shape[0]
    t = ATT_TILE
    row_chunk = 512
    return pl.pallas_call(
        functools.partial(_qkv_kernel, mod_base=mod_base, row_chunk=row_chunk, n_scaled=q_chunks),
        grid=(b, s // t),
        in_specs=[
            pl.BlockSpec((1, t, d), lambda i, j: (i, j, 0)),
            pl.BlockSpec((1, N_MOD, d), lambda i, j: (i, 0, 0)),
            _resident((1, d), lambda i, j: (0, 0)),
            _resident((n, d), lambda i, j: (0, 0)),
        ],
        out_specs=pl.BlockSpec((1, 1, n, t), lambda i, j: (i, j, 0, 0)),
        out_shape=jax.ShapeDtypeStruct((b, s // t, n, t), BF16),
        compiler_params=_cparams(2),
        name="qkv_proj",
    )(x, mod, g.reshape(1, d), w_in_t)


def _head_rmsnorm_t(o, g_col):
    return (o * lax.rsqrt(jnp.mean(o * o, axis=0, keepdims=True) + EPS)) * g_col


_CONTRACT_ROWS = (((0,), (0,)), ((), ()))


def _sb_kernel(q_ref, k_ref, v_ref, u_ref, g_ref, o_ref):
    nt, t = q_ref.shape[1], q_ref.shape[3]
    ks = u_ref.shape[0]
    sub = t // ks
    row = lax.broadcasted_iota(jnp.int32, (ks, t), 0)
    col = lax.broadcasted_iota(jnp.int32, (ks, t), 1)
    ucat = u_ref[...]
    g_col = g_ref[0]
    tiles = [(qi, kj) for qi in range(nt) for kj in range(qi, -1, -1)]
    subs = range(sub - 1, -1, -1)
    z_of, cat_of, incl_of, w_of = {}, {}, {}, {}
    state = {}

    def valid_mask(si):
        return row + (si * ks) < col

    def scores(n):
        qi, kj = tiles[n]
        q_t = q_ref[0, qi]
        z_of[n] = [lax.dot_general(k_ref[0, kj, :, si * ks:(si + 1) * ks], q_t, _CONTRACT_ROWS,
                                   preferred_element_type=F32) for si in range(sub)]

    def softplus(n):
        qi, kj = tiles[n]
        cats = []
        for si in range(sub):
            z = z_of[n][si]
            sp = jnp.maximum(z, 0.0) + jnp.log(1.0 + jnp.exp2(-jnp.abs(z))) * LOG2E
            if kj == qi:
                sp = jnp.where(valid_mask(si), sp, 0.0)
            hi = sp.astype(BF16)
            lo = (sp - hi.astype(F32)).astype(BF16)
            cats.append(jnp.concatenate([hi, lo], axis=0))
        cat_of[n] = cats

    def cumsum(n):
        incl_of[n] = [jnp.dot(ucat, c, preferred_element_type=F32) for c in cat_of.pop(n)]

    def weights(n):
        qi, kj = tiles[n]
        carry = state.setdefault(qi, [None, None])[0]
        zs, incls = z_of.pop(n), incl_of.pop(n)
        parts = [None] * sub
        for si in subs:
            arg = zs[si] - incls[si]
            if carry is not None:
                arg = arg - carry
            w = jnp.exp2(arg)
            if kj == qi:
                w = jnp.where(valid_mask(si), w, 0.0)
            parts[si] = w.astype(BF16)
            total = incls[si][0:1, :]
            carry = total if carry is None else carry + total
        state[qi][0] = carry
        w_of[n] = jnp.concatenate(parts, axis=0)

    def values(n):
        qi, kj = tiles[n]
        pv = jnp.dot(v_ref[0, kj], w_of.pop(n), preferred_element_type=F32)
        acc = state[qi][1]
        acc = pv if acc is None else acc + pv
        state[qi][1] = acc
        if kj == 0:
            o_ref[0, qi] = _head_rmsnorm_t(acc, g_col).astype(BF16)
            del state[qi]

    n_tiles = len(tiles)
    for step in range(n_tiles + 2):
        if step < n_tiles:
            scores(step)
        if 0 <= step - 1 < n_tiles:
            softplus(step - 1)
            cumsum(step - 1)
        if 0 <= step - 2 < n_tiles:
            weights(step - 2)
            values(step - 2)


def _suffix_matrix(n):
    u = (np.arange(n)[None, :] >= np.arange(n)[:, None]).astype(np.float32)
    return jnp.asarray(np.concatenate([u, u], axis=1), dtype=BF16)


def _sb_attention(qkv_t, g_out, q_blk, k_blk, v_blk):
    b, nt, _, t = qkv_t.shape
    nh = g_out.shape[0]
    blk = (1, nt, HEAD_DIM, t)
    return pl.pallas_call(
        _sb_kernel,
        grid=(b, nh),
        in_specs=[
            pl.BlockSpec(blk, lambda i, h: (i, 0, q_blk + h, 0)),
            pl.BlockSpec(blk, lambda i, h: (i, 0, k_blk + h, 0)),
            pl.BlockSpec(blk, lambda i, h: (i, 0, v_blk + h, 0)),
            _resident((SB_SUB_TILE, 2 * SB_SUB_TILE), lambda i, h: (0, 0)),
            pl.BlockSpec((1, HEAD_DIM, 1), lambda i, h: (h, 0, 0)),
        ],
        out_specs=pl.BlockSpec(blk, lambda i, h: (i, 0, h, 0)),
        out_shape=jax.ShapeDtypeStruct((b, nt, nh * HEAD_DIM, t), BF16),
        compiler_params=_cparams(2),
        name="stickbreak_attn",
    )(qkv_t, qkv_t, qkv_t, _suffix_matrix(SB_SUB_TILE), g_out.reshape(nh, HEAD_DIM, 1))


def _t5_causal_bucket(n, n_buckets):
    max_exact = n_buckets // 2
    nf = np.maximum(n, 1).astype(np.float32)
    large = max_exact + (np.log(nf / max_exact) / math.log(MAX_DISTANCE / max_exact)
                         * (n_buckets - max_exact)).astype(np.int32)
    large = np.minimum(large, n_buckets - 1)
    return np.where(n < max_exact, n, large).astype(np.int32)


def _dilated_bias_tiles(rel_bias, seq, t):
    dist = np.arange(seq)
    mult = np.zeros(seq, np.float32)
    for window, dilation in DIL_CONFIGS:
        mult += ((dist % dilation == 0) & (dist <= window)).astype(np.float32)
    bucket = _t5_causal_bucket(dist, rel_bias.shape[0])
    per_dist = (rel_bias[bucket].astype(F32) + jnp.log(jnp.maximum(jnp.asarray(mult), 1.0))[:, None]) * LOG2E
    per_dist = jnp.where(jnp.asarray(mult > 0)[:, None], per_dist, NEG_INF)
    nh = rel_bias.shape[1]
    table = jnp.concatenate([jnp.full((nh, t), NEG_INF, F32), per_dist.T], axis=1).reshape(nh, 1, seq + t)
    return pl.pallas_call(
        _bias_tile_kernel,
        grid=(nh,),
        in_specs=[pl.BlockSpec((1, 1, seq + t), lambda h: (h, 0, 0))],
        out_specs=pl.BlockSpec((1, seq // t, t, t), lambda h: (h, 0, 0, 0)),
        out_shape=jax.ShapeDtypeStruct((nh, seq // t, t, t), F32),
        compiler_params=_cparams(1),
        name="dilated_bias_tiles",
    )(table)


def _bias_tile_kernel(tab_ref, o_ref):
    no, t = o_ref.shape[1], o_ref.shape[2]
    for o in range(no):
        window = jnp.broadcast_to(tab_ref[0, :, o * t:(o + 2) * t], (t, 2 * t))
        skewed = pltpu.roll(window, 0, 1, stride=1, stride_axis=0)
        o_ref[0, o] = skewed[:, t:]


def _dil_kernel(q_ref, k_ref, v_ref, bias_ref, g_ref, o_ref):
    nt = q_ref.shape[1]
    g_col = g_ref[0]
    z_of = {}

    def scores(qi):
        q_t = q_ref[0, qi]
        zs = []
        m = None
        for kj in range(qi + 1):
            z = lax.dot_general(k_ref[0, kj], q_t, _CONTRACT_ROWS, preferred_element_type=F32)
            z = z + bias_ref[0, qi - kj]
            zs.append(z)
            zm = jnp.max(z, axis=0, keepdims=True)
            m = zm if m is None else jnp.maximum(m, zm)
        z_of[qi] = (zs, m)

    def attend(qi):
        zs, m = z_of.pop(qi)
        l = None
        acc = None
        for kj in range(qi + 1):
            p = jnp.exp2(zs[kj] - m)
            ps = jnp.sum(p, axis=0, keepdims=True)
            pv = jnp.dot(v_ref[0, kj], p.astype(BF16), preferred_element_type=F32)
            l = ps if l is None else l + ps
            acc = pv if acc is None else acc + pv
        o_ref[0, qi] = _head_rmsnorm_t(acc / l, g_col).astype(BF16)

    scores(0)
    for qi in range(nt):
        if qi + 1 < nt:
            scores(qi + 1)
        attend(qi)


def _dil_attention(qkv_t, bias_tiles, g_out, q_blk, k_blk, v_blk):
    b, nt, _, t = qkv_t.shape
    nh = g_out.shape[0]
    blk = (1, nt, HEAD_DIM, t)
    return pl.pallas_call(
        _dil_kernel,
        grid=(nh, b),
        in_specs=[
            pl.BlockSpec(blk, lambda h, i: (i, 0, q_blk + h, 0)),
            pl.BlockSpec(blk, lambda h, i: (i, 0, k_blk + h, 0)),
            pl.BlockSpec(blk, lambda h, i: (i, 0, v_blk + h, 0)),
            pl.BlockSpec((1, nt, t, t), lambda h, i: (h, 0, 0, 0)),
            pl.BlockSpec((1, HEAD_DIM, 1), lambda h, i: (h, 0, 0)),
        ],
        out_specs=pl.BlockSpec(blk, lambda h, i: (i, 0, h, 0)),
        out_shape=jax.ShapeDtypeStruct((b, nt, nh * HEAD_DIM, t), BF16),
        compiler_params=_cparams(2),
        name="dilated_attn",
    )(qkv_t, qkv_t, qkv_t, bias_tiles, g_out.reshape(nh, HEAD_DIM, 1))


def kernel(x, c, w_ada, b_ada, g_ffn1, w1_gate, w1_up, w1_down, g_mix, w_in, g_sb_out, g_dil_out, w_out, rel_bias, g_ffn2, w2_gate, w2_up, w2_down, g_final):
    depth = w_ada.shape[0]
    seq = x.shape[1]
    nh_sb = g_sb_out.shape[1]
    nh_dil = g_dil_out.shape[1]
    d_sb = nh_sb * HEAD_DIM
    sb_blk = (0, nh_sb, 2 * nh_sb)
    dil_blk = (3 * nh_sb, 3 * nh_sb + nh_dil, 3 * nh_sb + 2 * nh_dil)
    q_chunks = (0, 3 * d_sb // 512)
    bias_tiles = _dilated_bias_tiles(rel_bias, seq, ATT_TILE)
    for l in range(depth):
        mod = _adaln_mod(c, w_ada[l], b_ada[l])
        x = _ffn(x, mod, g_ffn1[l], w1_gate[l].astype(BF16), w1_up[l].astype(BF16), w1_down[l].astype(BF16), 0)
        qkv_t = _qkv(x, mod, g_mix[l], w_in[l].T.astype(BF16), 3, q_chunks)
        o_sb = _sb_attention(qkv_t, g_sb_out[l], *sb_blk)
        o_dil = _dil_attention(qkv_t, bias_tiles, g_dil_out[l], *dil_blk)
        w_o = w_out[l].astype(BF16)
        last = l == depth - 1
        x = _ffn(x, mod, g_ffn2[l], w2_gate[l].astype(BF16), w2_up[l].astype(BF16), w2_down[l].astype(BF16), 6,
                 mixer=(o_sb, o_dil, w_o[:d_sb], w_o[d_sb:], 3), g_final=g_final if last else None)
    return x
```

```python
import functools
import math

import jax
import jax.numpy as jnp
import numpy as np
from jax import lax
from jax.experimental import pallas as pl
from jax.experimental.pallas import tpu as pltpu

F32 = jnp.float32
BF16 = jnp.bfloat16

HEAD_DIM = 64
DIL_CONFIGS = ((128, 1), (512, 4), (2048, 16))
MAX_DISTANCE = 2048
N_MOD = 9
EPS = 1e-6
NEG_INF = -1e30

ATT_TILE = 256
SB_SUB_TILE = 128
HEADS_PER_STEP = 2
LOG2E = 1.4426950408889634
FFN_TOKEN_TILE = 512
PROJ_ROW_CHUNK = 512
VMEM_LIMIT = 56 * 1024 * 1024


def _cparams(n_grid):
    return pltpu.CompilerParams(dimension_semantics=("arbitrary",) * n_grid, vmem_limit_bytes=VMEM_LIMIT)


def _resident(block_shape, index_map):
    return pl.BlockSpec(block_shape, index_map, pipeline_mode=pl.Buffered(1))


def _mod_kernel(c_ref, w_ref, b_ref, o_ref):
    c = c_ref[...]
    s = c * jax.nn.sigmoid(c)
    o_ref[...] = jnp.dot(s.astype(BF16), w_ref[...].astype(BF16), preferred_element_type=F32) + b_ref[...]


def _adaln_mod(c, w_ada, b_ada):
    b, d = c.shape
    n = w_ada.shape[1]
    tn = 1024
    out = pl.pallas_call(
        _mod_kernel,
        grid=(n // tn,),
        in_specs=[
            pl.BlockSpec((b, d), lambda j: (0, 0)),
            pl.BlockSpec((d, tn), lambda j: (0, j)),
            pl.BlockSpec((1, tn), lambda j: (0, j)),
        ],
        out_specs=pl.BlockSpec((b, tn), lambda j: (0, j)),
        out_shape=jax.ShapeDtypeStruct((b, n), F32),
        compiler_params=_cparams(1),
        name="adaln_mod",
    )(c, w_ada, b_ada.reshape(1, n))
    return out.reshape(b, N_MOD, d)


def _modulate(x, g, shift, scale):
    y = x * lax.rsqrt(jnp.mean(x * x, axis=-1, keepdims=True) + EPS)
    return (y * g) * (1.0 + scale) + shift


def _ffn_kernel(*refs, mod_base, mixer_base, proj_base, q_chunks, final):
    x_ref, mod_ref, g_ref, wg_ref, wu_ref, wd_ref = refs[:6]
    rest = list(refs[6:])
    qkv_ref = rest.pop() if proj_base is not None else None
    o_ref = rest.pop()
    gf_ref = rest.pop() if final else None
    gmix_ref, win_ref = (rest.pop(-2), rest.pop()) if proj_base is not None else (None, None)
    mixer_refs = rest if mixer_base is not None else None
    shift = mod_ref[0, mod_base:mod_base + 1, :]
    scale = mod_ref[0, mod_base + 1:mod_base + 2, :]
    gate = mod_ref[0, mod_base + 2:mod_base + 3, :]
    sub = ATT_TILE
    n_sub = x_ref.shape[1] // sub
    xs, hs, acts = {}, {}, {}

    def prologue(i):
        x = x_ref[0, i * sub:(i + 1) * sub, :]
        if mixer_refs is not None:
            osb_ref, odil_ref, wsb_ref, wdil_ref = mixer_refs
            y = lax.dot_general(osb_ref[0, i], wsb_ref[...], _CONTRACT_ROWS, preferred_element_type=F32)
            y = y + lax.dot_general(odil_ref[0, i], wdil_ref[...], _CONTRACT_ROWS, preferred_element_type=F32)
            x = x + mod_ref[0, mixer_base + 2:mixer_base + 3, :] * y
        xs[i] = x
        hs[i] = _modulate(x, g_ref[...], shift, scale).astype(BF16)

    def expand(i):
        hb = hs.pop(i)
        gg = jnp.dot(hb, wg_ref[...], preferred_element_type=F32)
        uu = jnp.dot(hb, wu_ref[...], preferred_element_type=F32)
        acts[i] = ((gg * jax.nn.sigmoid(gg)) * uu).astype(BF16)

    def contract(i):
        out = xs.pop(i) + (0.5 * gate) * jnp.dot(acts.pop(i), wd_ref[...], preferred_element_type=F32)
        if proj_base is not None:
            xs[i] = out
        if final:
            out = out * lax.rsqrt(jnp.mean(out * out, axis=-1, keepdims=True) + EPS) * gf_ref[...]
        o_ref[0, i * sub:(i + 1) * sub, :] = out

    def project(i):
        hb = _modulate(xs.pop(i), gmix_ref[...], mod_ref[0, proj_base:proj_base + 1, :],
                       mod_ref[0, proj_base + 1:proj_base + 2, :]).astype(BF16)
        q_scale = HEAD_DIM ** -0.5 * LOG2E
        for r in range(win_ref.shape[0] // PROJ_ROW_CHUNK):
            rows = slice(r * PROJ_ROW_CHUNK, (r + 1) * PROJ_ROW_CHUNK)
            res = lax.dot_general(win_ref[rows, :], hb, (((1,), (1,)), ((), ())), preferred_element_type=F32)
            if r in q_chunks:
                res = res * q_scale
            qkv_ref[0, i, rows, :] = res.astype(BF16)

    for i in range(n_sub):
        prologue(i)
    for i in range(n_sub):
        expand(i)
    for i in range(n_sub):
        contract(i)
    if proj_base is not None:
        for i in range(n_sub):
            project(i)


def _ffn(x, mod, g, wg, wu, wd, mod_base, mixer=None, proj=None, g_final=None):
    b, s, d = x.shape
    dff = wg.shape[1]
    tm = FFN_TOKEN_TILE
    final = g_final is not None
    in_specs = [
        pl.BlockSpec((1, tm, d), lambda i, j: (i, j, 0)),
        pl.BlockSpec((1, N_MOD, d), lambda i, j: (i, 0, 0)),
        _resident((1, d), lambda i, j: (0, 0)),
        _resident((d, dff), lambda i, j: (0, 0)),
        _resident((d, dff), lambda i, j: (0, 0)),
        _resident((dff, d), lambda i, j: (0, 0)),
    ]
    args = [x, mod, g.reshape(1, d), wg, wu, wd]
    mixer_base = None
    if mixer is not None:
        o_sb, o_dil, w_sb, w_dil, mixer_base = mixer
        t = o_sb.shape[3]
        for o_heads in (o_sb, o_dil):
            in_specs.append(pl.BlockSpec((1, tm // t, o_heads.shape[2], t), lambda i, j: (i, j, 0, 0)))
            args.append(o_heads)
        for w_heads in (w_sb, w_dil):
            in_specs.append(_resident(w_heads.shape, lambda i, j: (0, 0)))
            args.append(w_heads)
    out_specs = pl.BlockSpec((1, tm, d), lambda i, j: (i, j, 0))
    out_shape = jax.ShapeDtypeStruct((b, s, d), F32)
    proj_base, q_chunks = None, ()
    if proj is not None:
        g_mix, w_in_t, proj_base, q_chunks = proj
        n = w_in_t.shape[0]
        in_specs += [_resident((1, d), lambda i, j: (0, 0)), _resident((n, d), lambda i, j: (0, 0))]
        args += [g_mix.reshape(1, d), w_in_t]
        out_specs = (out_specs, pl.BlockSpec((1, tm // ATT_TILE, n, ATT_TILE), lambda i, j: (i, j, 0, 0)))
        out_shape = (out_shape, jax.ShapeDtypeStruct((b, s // ATT_TILE, n, ATT_TILE), BF16))
    if final:
        in_specs.append(_resident((1, d), lambda i, j: (0, 0)))
        args.append(g_final.reshape(1, d))
    return pl.pallas_call(
        functools.partial(_ffn_kernel, mod_base=mod_base, mixer_base=mixer_base, proj_base=proj_base,
                          q_chunks=q_chunks, final=final),
        grid=(b, s // tm),
        in_specs=in_specs,
        out_specs=out_specs,
        out_shape=out_shape,
        compiler_params=_cparams(2),
        name="ffn_mixer_out" if mixer is not None else "ffn_qkv" if proj is not None else "ffn",
    )(*args)


def _head_rmsnorm_t(o, g_col):
    return (o * lax.rsqrt(jnp.mean(o * o, axis=0, keepdims=True) + EPS)) * g_col


_CONTRACT_ROWS = (((0,), (0,)), ((), ()))


def _sb_kernel(q_ref, k_ref, v_ref, u_ref, g_ref, o_ref):
    nt, t = q_ref.shape[1], q_ref.shape[3]
    n_heads = q_ref.shape[2] // HEAD_DIM
    ks = u_ref.shape[0]
    sub = t // ks
    ucat = u_ref[...]
    tiles = [(h, qi, kj) for h in range(n_heads) for qi in range(nt) for kj in range(qi, -1, -1)]
    subs = range(sub - 1, -1, -1)
    z_of, cat_of, incl_of, w_of = {}, {}, {}, {}
    state = {}

    def head_rows(h):
        return slice(h * HEAD_DIM, (h + 1) * HEAD_DIM)

    def first_lane(qi, kj, si):
        return si * ks if kj == qi else 0

    def causal(lanes):
        return lax.broadcasted_iota(jnp.int32, (ks, lanes), 0) < lax.broadcasted_iota(jnp.int32, (ks, lanes), 1)

    def scores(n):
        h, qi, kj = tiles[n]
        q_t = q_ref[0, qi, head_rows(h), :]
        z_of[n] = [lax.dot_general(k_ref[0, kj, head_rows(h), si * ks:(si + 1) * ks],
                                   q_t[:, first_lane(qi, kj, si):], _CONTRACT_ROWS,
                                   preferred_element_type=F32) for si in range(sub)]

    def softplus(n):
        h, qi, kj = tiles[n]
        cats = []
        for z in z_of[n]:
            sp = jnp.maximum(z, 0.0) + jnp.log(1.0 + jnp.exp2(-jnp.abs(z))) * LOG2E
            if kj == qi:
                sp = jnp.where(causal(z.shape[1]), sp, 0.0)
            hi = sp.astype(BF16)
            lo = (sp - hi.astype(F32)).astype(BF16)
            cats.append(jnp.concatenate([hi, lo], axis=0))
        cat_of[n] = cats

    def cumsum(n):
        incl_of[n] = [jnp.dot(ucat, c, preferred_element_type=F32) for c in cat_of.pop(n)]

    def weights(n):
        h, qi, kj = tiles[n]
        carry = state.setdefault((h, qi), [None, None])[0]
        zs, incls = z_of.pop(n), incl_of.pop(n)
        parts = [None] * sub
        for si in subs:
            skipped = first_lane(qi, kj, si)
            arg = zs[si] - incls[si]
            if carry is not None:
                arg = arg - carry[:, skipped:]
            w = jnp.exp2(arg)
            total = incls[si][0:1, :]
            if kj == qi:
                w = jnp.where(causal(w.shape[1]), w, 0.0)
            if skipped:
                w = jnp.concatenate([jnp.zeros((ks, skipped), F32), w], axis=1)
                total = jnp.concatenate([jnp.zeros((1, skipped), F32), total], axis=1)
            parts[si] = w.astype(BF16)
            carry = total if carry is None else carry + total
        state[h, qi][0] = carry
        w_of[n] = jnp.concatenate(parts, axis=0)

    def values(n):
        h, qi, kj = tiles[n]
        pv = jnp.dot(v_ref[0, kj, head_rows(h), :], w_of.pop(n), preferred_element_type=F32)
        acc = state[h, qi][1]
        acc = pv if acc is None else acc + pv
        state[h, qi][1] = acc
        if kj == 0:
            o_ref[0, qi, head_rows(h), :] = _head_rmsnorm_t(acc, g_ref[h]).astype(BF16)
            del state[h, qi]

    n_tiles = len(tiles)
    for step in range(n_tiles + 2):
        if step < n_tiles:
            scores(step)
        if 0 <= step - 1 < n_tiles:
            softplus(step - 1)
            cumsum(step - 1)
        if 0 <= step - 2 < n_tiles:
            weights(step - 2)
            values(step - 2)


def _suffix_matrix(n):
    u = (np.arange(n)[None, :] >= np.arange(n)[:, None]).astype(np.float32)
    return jnp.asarray(np.concatenate([u, u], axis=1), dtype=BF16)


def _sb_attention(qkv_t, g_out, q_blk, k_blk, v_blk):
    b, nt, _, t = qkv_t.shape
    nh = g_out.shape[0]
    hps = HEADS_PER_STEP
    blk = (1, nt, hps * HEAD_DIM, t)
    return pl.pallas_call(
        _sb_kernel,
        grid=(b, nh // hps),
        in_specs=[
            pl.BlockSpec(blk, lambda i, h: (i, 0, q_blk // hps + h, 0)),
            pl.BlockSpec(blk, lambda i, h: (i, 0, k_blk // hps + h, 0)),
            pl.BlockSpec(blk, lambda i, h: (i, 0, v_blk // hps + h, 0)),
            _resident((SB_SUB_TILE, 2 * SB_SUB_TILE), lambda i, h: (0, 0)),
            pl.BlockSpec((hps, HEAD_DIM, 1), lambda i, h: (h, 0, 0)),
        ],
        out_specs=pl.BlockSpec(blk, lambda i, h: (i, 0, h, 0)),
        out_shape=jax.ShapeDtypeStruct((b, nt, nh * HEAD_DIM, t), BF16),
        compiler_params=_cparams(2),
        name="stickbreak_attn",
    )(qkv_t, qkv_t, qkv_t, _suffix_matrix(SB_SUB_TILE), g_out.reshape(nh, HEAD_DIM, 1))


def _t5_causal_bucket(n, n_buckets):
    max_exact = n_buckets // 2
    nf = np.maximum(n, 1).astype(np.float32)
    large = max_exact + (np.log(nf / max_exact) / math.log(MAX_DISTANCE / max_exact)
                         * (n_buckets - max_exact)).astype(np.int32)
    large = np.minimum(large, n_buckets - 1)
    return np.where(n < max_exact, n, large).astype(np.int32)


def _dilated_bias_tiles(rel_bias, seq, t):
    dist = np.arange(seq)
    mult = np.zeros(seq, np.float32)
    for window, dilation in DIL_CONFIGS:
        mult += ((dist % dilation == 0) & (dist <= window)).astype(np.float32)
    bucket = _t5_causal_bucket(dist, rel_bias.shape[0])
    per_dist = (rel_bias[bucket].astype(F32) + jnp.log(jnp.maximum(jnp.asarray(mult), 1.0))[:, None]) * LOG2E
    per_dist = jnp.where(jnp.asarray(mult > 0)[:, None], per_dist, NEG_INF)
    nh = rel_bias.shape[1]
    table = jnp.concatenate([jnp.full((nh, t), NEG_INF, F32), per_dist.T], axis=1).reshape(nh, 1, seq + t)
    return pl.pallas_call(
        _bias_tile_kernel,
        grid=(nh,),
        in_specs=[pl.BlockSpec((1, 1, seq + t), lambda h: (h, 0, 0))],
        out_specs=pl.BlockSpec((1, seq // t, t, t), lambda h: (h, 0, 0, 0)),
        out_shape=jax.ShapeDtypeStruct((nh, seq // t, t, t), F32),
        compiler_params=_cparams(1),
        name="dilated_bias_tiles",
    )(table)


def _bias_tile_kernel(tab_ref, o_ref):
    no, t = o_ref.shape[1], o_ref.shape[2]
    for o in range(no):
        window = jnp.broadcast_to(tab_ref[0, :, o * t:(o + 2) * t], (t, 2 * t))
        skewed = pltpu.roll(window, 0, 1, stride=1, stride_axis=0)
        o_ref[0, o] = skewed[:, t:]


def _dil_kernel(q_ref, k_ref, v_ref, bias_ref, g_ref, o_ref):
    nt, t = q_ref.shape[1], q_ref.shape[3]
    n_heads = q_ref.shape[2] // HEAD_DIM
    units = [(h, qi) for h in range(n_heads) for qi in range(nt)]
    ones_rows = jnp.ones((16, t), BF16)
    z_of = {}

    def head_rows(h):
        return slice(h * HEAD_DIM, (h + 1) * HEAD_DIM)

    def scores(u):
        h, qi = units[u]
        q_t = q_ref[0, qi, head_rows(h), :]
        zs = []
        m = None
        for kj in range(qi + 1):
            z = lax.dot_general(k_ref[0, kj, head_rows(h), :], q_t, _CONTRACT_ROWS,
                                preferred_element_type=F32)
            z = z + bias_ref[h, qi - kj]
            zs.append(z)
            zm = jnp.max(z, axis=0, keepdims=True)
            m = zm if m is None else jnp.maximum(m, zm)
        z_of[u] = (zs, m)

    def attend(u):
        h, qi = units[u]
        zs, m = z_of.pop(u)
        acc = None
        for kj in range(qi + 1):
            p = jnp.exp2(zs[kj] - m).astype(BF16)
            v_ext = jnp.concatenate([v_ref[0, kj, head_rows(h), :], ones_rows], axis=0)
            pv = jnp.dot(v_ext, p, preferred_element_type=F32)
            acc = pv if acc is None else acc + pv
        out = acc[:HEAD_DIM] / acc[HEAD_DIM:HEAD_DIM + 1]
        o_ref[0, qi, head_rows(h), :] = _head_rmsnorm_t(out, g_ref[h]).astype(BF16)

    scores(0)
    for u in range(len(units)):
        if u + 1 < len(units):
            scores(u + 1)
        attend(u)


def _dil_attention(qkv_t, bias_tiles, g_out, q_blk, k_blk, v_blk):
    b, nt, _, t = qkv_t.shape
    nh = g_out.shape[0]
    hps = HEADS_PER_STEP
    blk = (1, nt, hps * HEAD_DIM, t)
    return pl.pallas_call(
        _dil_kernel,
        grid=(nh // hps, b),
        in_specs=[
            pl.BlockSpec(blk, lambda h, i: (i, 0, q_blk // hps + h, 0)),
            pl.BlockSpec(blk, lambda h, i: (i, 0, k_blk // hps + h, 0)),
            pl.BlockSpec(blk, lambda h, i: (i, 0, v_blk // hps + h, 0)),
            pl.BlockSpec((hps, nt, t, t), lambda h, i: (h, 0, 0, 0)),
            pl.BlockSpec((hps, HEAD_DIM, 1), lambda h, i: (h, 0, 0)),
        ],
        out_specs=pl.BlockSpec(blk, lambda h, i: (i, 0, h, 0)),
        out_shape=jax.ShapeDtypeStruct((b, nt, nh * HEAD_DIM, t), BF16),
        compiler_params=_cparams(2),
        name="dilated_attn",
    )(qkv_t, qkv_t, qkv_t, bias_tiles, g_out.reshape(nh, HEAD_DIM, 1))


def kernel(x, c, w_ada, b_ada, g_ffn1, w1_gate, w1_up, w1_down, g_mix, w_in, g_sb_out, g_dil_out, w_out, rel_bias, g_ffn2, w2_gate, w2_up, w2_down, g_final):
    depth = w_ada.shape[0]
    seq = x.shape[1]
    nh_sb = g_sb_out.shape[1]
    nh_dil = g_dil_out.shape[1]
    d_sb = nh_sb * HEAD_DIM
    sb_blk = (0, nh_sb, 2 * nh_sb)
    dil_blk = (3 * nh_sb, 3 * nh_sb + nh_dil, 3 * nh_sb + 2 * nh_dil)
    d_dil = nh_dil * HEAD_DIM
    assert d_sb % PROJ_ROW_CHUNK == 0 and d_dil % PROJ_ROW_CHUNK == 0
    q_chunks = tuple(range(d_sb // PROJ_ROW_CHUNK)) + tuple(
        range(3 * d_sb // PROJ_ROW_CHUNK, (3 * d_sb + d_dil) // PROJ_ROW_CHUNK))
    bias_tiles = _dilated_bias_tiles(rel_bias, seq, ATT_TILE)
    for l in range(depth):
        mod = _adaln_mod(c, w_ada[l], b_ada[l])
        x, qkv_t = _ffn(x, mod, g_ffn1[l], w1_gate[l].astype(BF16), w1_up[l].astype(BF16), w1_down[l].astype(BF16), 0,
                        proj=(g_mix[l], w_in[l].T.astype(BF16), 3, q_chunks))
        o_sb = _sb_attention(qkv_t, g_sb_out[l], *sb_blk)
        o_dil = _dil_attention(qkv_t, bias_tiles, g_dil_out[l], *dil_blk)
        w_o = w_out[l].astype(BF16)
        last = l == depth - 1
        x = _ffn(x, mod, g_ffn2[l], w2_gate[l].astype(BF16), w2_up[l].astype(BF16), w2_down[l].astype(BF16), 6,
                 mixer=(o_sb, o_dil, w_o[:d_sb], w_o[d_sb:], 3), g_final=g_final if last else None)
    return x
```

```python
import functools
import math

import jax
import jax.numpy as jnp
import numpy as np
from jax import lax
from jax.experimental import pallas as pl
from jax.experimental.pallas import tpu as pltpu

F32 = jnp.float32
BF16 = jnp.bfloat16

HEAD_DIM = 64
DIL_CONFIGS = ((128, 1), (512, 4), (2048, 16))
MAX_DISTANCE = 2048
N_MOD = 9
EPS = 1e-6
NEG_INF = -1e30

ATT_TILE = 256
SB_SUB_TILE = 128
HEADS_PER_STEP = 2
SB_STAGE_LEADS = (1, 1)
DIL_SCORE_LEAD = 4
LOG2E = 1.4426950408889634
FFN_TOKEN_TILE = 512
PROJ_ROW_CHUNK = 512
VMEM_LIMIT = 56 * 1024 * 1024


def _cparams(n_grid):
    return pltpu.CompilerParams(dimension_semantics=("arbitrary",) * n_grid, vmem_limit_bytes=VMEM_LIMIT)


def _resident(block_shape, index_map):
    return pl.BlockSpec(block_shape, index_map, pipeline_mode=pl.Buffered(1))


def _mod_kernel(c_ref, w_ref, b_ref, o_ref):
    c = c_ref[...]
    s = c * jax.nn.sigmoid(c)
    o_ref[...] = jnp.dot(s.astype(BF16), w_ref[...].astype(BF16), preferred_element_type=F32) + b_ref[...]


def _adaln_mod(c, w_ada, b_ada):
    b, d = c.shape
    n = w_ada.shape[1]
    tn = 1024
    out = pl.pallas_call(
        _mod_kernel,
        grid=(n // tn,),
        in_specs=[
            pl.BlockSpec((b, d), lambda j: (0, 0)),
            pl.BlockSpec((d, tn), lambda j: (0, j)),
            pl.BlockSpec((1, tn), lambda j: (0, j)),
        ],
        out_specs=pl.BlockSpec((b, tn), lambda j: (0, j)),
        out_shape=jax.ShapeDtypeStruct((b, n), F32),
        compiler_params=_cparams(1),
        name="adaln_mod",
    )(c, w_ada, b_ada.reshape(1, n))
    return out.reshape(b, N_MOD, d)


def _modulate(x, g, shift, scale):
    y = x * lax.rsqrt(jnp.mean(x * x, axis=-1, keepdims=True) + EPS)
    return (y * g) * (1.0 + scale) + shift


def _ffn_kernel(*refs, mod_base, mixer_base, proj_base, q_chunks, final):
    x_ref, mod_ref, g_ref, wg_ref, wu_ref, wd_ref = refs[:6]
    rest = list(refs[6:])
    qkv_ref = rest.pop() if proj_base is not None else None
    o_ref = rest.pop()
    gf_ref = rest.pop() if final else None
    gmix_ref, win_ref = (rest.pop(-2), rest.pop()) if proj_base is not None else (None, None)
    mixer_refs = rest if mixer_base is not None else None
    shift = mod_ref[0, mod_base:mod_base + 1, :]
    scale = mod_ref[0, mod_base + 1:mod_base + 2, :]
    gate = mod_ref[0, mod_base + 2:mod_base + 3, :]
    sub = ATT_TILE
    n_sub = x_ref.shape[1] // sub
    xs, hs, acts = {}, {}, {}

    def prologue(i):
        x = x_ref[0, i * sub:(i + 1) * sub, :]
        if mixer_refs is not None:
            osb_ref, odil_ref, wsb_ref, wdil_ref = mixer_refs
            y = lax.dot_general(osb_ref[0, i], wsb_ref[...], _CONTRACT_ROWS, preferred_element_type=F32)
            y = y + lax.dot_general(odil_ref[0, i], wdil_ref[...], _CONTRACT_ROWS, preferred_element_type=F32)
            x = x + mod_ref[0, mixer_base + 2:mixer_base + 3, :] * y
        xs[i] = x
        hs[i] = _modulate(x, g_ref[...], shift, scale).astype(BF16)

    def expand(i):
        hb = hs.pop(i)
        gg = jnp.dot(hb, wg_ref[...], preferred_element_type=F32)
        uu = jnp.dot(hb, wu_ref[...], preferred_element_type=F32)
        acts[i] = ((gg * jax.nn.sigmoid(gg)) * uu).astype(BF16)

    def contract(i):
        out = xs.pop(i) + (0.5 * gate) * jnp.dot(acts.pop(i), wd_ref[...], preferred_element_type=F32)
        if proj_base is not None:
            xs[i] = out
        if final:
            out = out * lax.rsqrt(jnp.mean(out * out, axis=-1, keepdims=True) + EPS) * gf_ref[...]
        o_ref[0, i * sub:(i + 1) * sub, :] = out

    def project(i):
        hb = _modulate(xs.pop(i), gmix_ref[...], mod_ref[0, proj_base:proj_base + 1, :],
                       mod_ref[0, proj_base + 1:proj_base + 2, :]).astype(BF16)
        q_scale = HEAD_DIM ** -0.5 * LOG2E
        for r in range(win_ref.shape[0] // PROJ_ROW_CHUNK):
            rows = slice(r * PROJ_ROW_CHUNK, (r + 1) * PROJ_ROW_CHUNK)
            res = lax.dot_general(win_ref[rows, :], hb, (((1,), (1,)), ((), ())), preferred_element_type=F32)
            if r in q_chunks:
                res = res * q_scale
            qkv_ref[0, i, rows, :] = res.astype(BF16)

    for i in range(n_sub):
        prologue(i)
    for i in range(n_sub):
        expand(i)
    for i in range(n_sub):
        contract(i)
    if proj_base is not None:
        for i in range(n_sub):
            project(i)


def _ffn(x, mod, g, wg, wu, wd, mod_base, mixer=None, proj=None, g_final=None):
    b, s, d = x.shape
    dff = wg.shape[1]
    tm = FFN_TOKEN_TILE
    final = g_final is not None
    in_specs = [
        pl.BlockSpec((1, tm, d), lambda i, j: (i, j, 0)),
        pl.BlockSpec((1, N_MOD, d), lambda i, j: (i, 0, 0)),
        _resident((1, d), lambda i, j: (0, 0)),
        _resident((d, dff), lambda i, j: (0, 0)),
        _resident((d, dff), lambda i, j: (0, 0)),
        _resident((dff, d), lambda i, j: (0, 0)),
    ]
    args = [x, mod, g.reshape(1, d), wg, wu, wd]
    mixer_base = None
    if mixer is not None:
        o_sb, o_dil, w_sb, w_dil, mixer_base = mixer
        t = o_sb.shape[3]
        for o_heads in (o_sb, o_dil):
            in_specs.append(pl.BlockSpec((1, tm // t, o_heads.shape[2], t), lambda i, j: (i, j, 0, 0)))
            args.append(o_heads)
        for w_heads in (w_sb, w_dil):
            in_specs.append(_resident(w_heads.shape, lambda i, j: (0, 0)))
            args.append(w_heads)
    out_specs = pl.BlockSpec((1, tm, d), lambda i, j: (i, j, 0))
    out_shape = jax.ShapeDtypeStruct((b, s, d), F32)
    proj_base, q_chunks = None, ()
    if proj is not None:
        g_mix, w_in_t, proj_base, q_chunks = proj
        n = w_in_t.shape[0]
        in_specs += [_resident((1, d), lambda i, j: (0, 0)), _resident((n, d), lambda i, j: (0, 0))]
        args += [g_mix.reshape(1, d), w_in_t]
        out_specs = (out_specs, pl.BlockSpec((1, tm // ATT_TILE, n, ATT_TILE), lambda i, j: (i, j, 0, 0)))
        out_shape = (out_shape, jax.ShapeDtypeStruct((b, s // ATT_TILE, n, ATT_TILE), BF16))
    if final:
        in_specs.append(_resident((1, d), lambda i, j: (0, 0)))
        args.append(g_final.reshape(1, d))
    return pl.pallas_call(
        functools.partial(_ffn_kernel, mod_base=mod_base, mixer_base=mixer_base, proj_base=proj_base,
                          q_chunks=q_chunks, final=final),
        grid=(b, s // tm),
        in_specs=in_specs,
        out_specs=out_specs,
        out_shape=out_shape,
        compiler_params=_cparams(2),
        name="ffn_mixer_out" if mixer is not None else "ffn_qkv" if proj is not None else "ffn",
    )(*args)


def _head_rmsnorm_t(o, g_col):
    return (o * lax.rsqrt(jnp.mean(o * o, axis=0, keepdims=True) + EPS)) * g_col


_CONTRACT_ROWS = (((0,), (0,)), ((), ()))


def _sb_program(q_ref, k_ref, v_ref, u_ref, g_ref, o_ref):
    nt, t = q_ref.shape[1], q_ref.shape[3]
    n_heads = q_ref.shape[2] // HEAD_DIM
    ks = u_ref.shape[0]
    sub = t // ks
    ucat = u_ref[...]
    tiles = [(h, qi, kj) for h in range(n_heads) for qi in range(nt) for kj in range(qi, -1, -1)]
    subs = range(sub - 1, -1, -1)
    z_of, cat_of, incl_of, w_of = {}, {}, {}, {}
    state = {}

    def head_rows(h):
        return slice(h * HEAD_DIM, (h + 1) * HEAD_DIM)

    def first_lane(qi, kj, si):
        return si * ks if kj == qi else 0

    def causal(lanes):
        return lax.broadcasted_iota(jnp.int32, (ks, lanes), 0) < lax.broadcasted_iota(jnp.int32, (ks, lanes), 1)

    def scores(n):
        h, qi, kj = tiles[n]
        q_t = q_ref[0, qi, head_rows(h), :]
        z_of[n] = [lax.dot_general(k_ref[0, kj, head_rows(h), si * ks:(si + 1) * ks],
                                   q_t[:, first_lane(qi, kj, si):], _CONTRACT_ROWS,
                                   preferred_element_type=F32) for si in range(sub)]

    def softplus(n):
        h, qi, kj = tiles[n]
        cats = []
        for z in z_of[n]:
            sp = jnp.maximum(z, 0.0) + jnp.log(1.0 + jnp.exp2(-jnp.abs(z))) * LOG2E
            if kj == qi:
                sp = jnp.where(causal(z.shape[1]), sp, 0.0)
            hi = sp.astype(BF16)
            lo = (sp - hi.astype(F32)).astype(BF16)
            cats.append(jnp.concatenate([hi, lo], axis=0))
        cat_of[n] = cats

    def cumsum(n):
        incl_of[n] = [jnp.dot(ucat, c, preferred_element_type=F32) for c in cat_of.pop(n)]

    def weights(n):
        h, qi, kj = tiles[n]
        carry = state.setdefault((h, qi), [None, None])[0]
        zs, incls = z_of.pop(n), incl_of.pop(n)
        parts = [None] * sub
        for si in subs:
            skipped = first_lane(qi, kj, si)
            arg = zs[si] - incls[si]
            if carry is not None:
                arg = arg - carry[:, skipped:]
            w = jnp.exp2(arg)
            total = incls[si][0:1, :]
            if kj == qi:
                w = jnp.where(causal(w.shape[1]), w, 0.0)
            if skipped:
                w = jnp.concatenate([jnp.zeros((ks, skipped), F32), w], axis=1)
                total = jnp.concatenate([jnp.zeros((1, skipped), F32), total], axis=1)
            parts[si] = w.astype(BF16)
            carry = total if carry is None else carry + total
        state[h, qi][0] = carry
        w_of[n] = jnp.concatenate(parts, axis=0)

    def values(n):
        h, qi, kj = tiles[n]
        pv = jnp.dot(v_ref[0, kj, head_rows(h), :], w_of.pop(n), preferred_element_type=F32)
        acc = state[h, qi][1]
        acc = pv if acc is None else acc + pv
        state[h, qi][1] = acc
        if kj == 0:
            o_ref[0, qi, head_rows(h), :] = _head_rmsnorm_t(acc, g_ref[h]).astype(BF16)
            del state[h, qi]

    n_tiles = len(tiles)

    lead_a, lead_b = SB_STAGE_LEADS

    def step(s):
        if s < n_tiles:
            scores(s)
        if 0 <= s - lead_a < n_tiles:
            softplus(s - lead_a)
            cumsum(s - lead_a)
        if 0 <= s - lead_a - lead_b < n_tiles:
            weights(s - lead_a - lead_b)
            values(s - lead_a - lead_b)

    return [functools.partial(step, s) for s in range(n_tiles + lead_a + lead_b)]


def _suffix_matrix(n):
    u = (np.arange(n)[None, :] >= np.arange(n)[:, None]).astype(np.float32)
    return jnp.asarray(np.concatenate([u, u], axis=1), dtype=BF16)


def _t5_causal_bucket(n, n_buckets):
    max_exact = n_buckets // 2
    nf = np.maximum(n, 1).astype(np.float32)
    large = max_exact + (np.log(nf / max_exact) / math.log(MAX_DISTANCE / max_exact)
                         * (n_buckets - max_exact)).astype(np.int32)
    large = np.minimum(large, n_buckets - 1)
    return np.where(n < max_exact, n, large).astype(np.int32)


def _dilated_bias_tiles(rel_bias, seq, t):
    dist = np.arange(seq)
    mult = np.zeros(seq, np.float32)
    for window, dilation in DIL_CONFIGS:
        mult += ((dist % dilation == 0) & (dist <= window)).astype(np.float32)
    bucket = _t5_causal_bucket(dist, rel_bias.shape[0])
    per_dist = (rel_bias[bucket].astype(F32) + jnp.log(jnp.maximum(jnp.asarray(mult), 1.0))[:, None]) * LOG2E
    per_dist = jnp.where(jnp.asarray(mult > 0)[:, None], per_dist, NEG_INF)
    nh = rel_bias.shape[1]
    table = jnp.concatenate([jnp.full((nh, t), NEG_INF, F32), per_dist.T], axis=1).reshape(nh, 1, seq + t)
    return pl.pallas_call(
        _bias_tile_kernel,
        grid=(nh,),
        in_specs=[pl.BlockSpec((1, 1, seq + t), lambda h: (h, 0, 0))],
        out_specs=pl.BlockSpec((1, seq // t, t, t), lambda h: (h, 0, 0, 0)),
        out_shape=jax.ShapeDtypeStruct((nh, seq // t, t, t), F32),
        compiler_params=_cparams(1),
        name="dilated_bias_tiles",
    )(table)


def _bias_tile_kernel(tab_ref, o_ref):
    no, t = o_ref.shape[1], o_ref.shape[2]
    for o in range(no):
        window = jnp.broadcast_to(tab_ref[0, :, o * t:(o + 2) * t], (t, 2 * t))
        skewed = pltpu.roll(window, 0, 1, stride=1, stride_axis=0)
        o_ref[0, o] = skewed[:, t:]


def _dil_fast_program(q_ref, k_ref, v_ref, bias_ref, g_ref, o_ref):
    nt, t = q_ref.shape[1], q_ref.shape[3]
    n_heads = q_ref.shape[2] // HEAD_DIM
    ones_rows = jnp.ones((16, t), BF16)
    tiles = [(h, qi, kj) for h in range(n_heads) for qi in range(nt) for kj in range(qi, -1, -1)]
    z_of, state = {}, {}
    bad = [jnp.zeros((HEAD_DIM + 16, t), jnp.int32)]

    def head_rows(h):
        return slice(h * HEAD_DIM, (h + 1) * HEAD_DIM)

    def scores(n):
        h, qi, kj = tiles[n]
        z = lax.dot_general(k_ref[0, kj, head_rows(h), :], q_ref[0, qi, head_rows(h), :], _CONTRACT_ROWS,
                            preferred_element_type=F32)
        z_of[n] = z + bias_ref[h, qi - kj]

    def attend(n):
        h, qi, kj = tiles[n]
        z = z_of.pop(n)
        if kj == qi:
            state[h, qi] = [jnp.max(z, axis=0, keepdims=True), None]
        m, acc = state[h, qi]
        v_ext = jnp.concatenate([v_ref[0, kj, head_rows(h), :], ones_rows], axis=0)
        pv = jnp.dot(v_ext, jnp.exp2(z - m).astype(BF16), preferred_element_type=F32)
        acc = pv if acc is None else acc + pv
        state[h, qi][1] = acc
        if kj == 0:
            bad[0] = jnp.maximum(bad[0], jnp.where(jnp.isfinite(acc), 0, 1))
            out = acc[:HEAD_DIM] / acc[HEAD_DIM:HEAD_DIM + 1]
            o_ref[0, qi, head_rows(h), :] = _head_rmsnorm_t(out, g_ref[h]).astype(BF16)
            del state[h, qi]

    def step(s):
        if s < len(tiles):
            scores(s)
        if s >= DIL_SCORE_LEAD:
            attend(s - DIL_SCORE_LEAD)

    def overflowed():
        return jnp.max(bad[0])

    return [functools.partial(step, s) for s in range(len(tiles) + DIL_SCORE_LEAD)], overflowed


def _dil_exact_program(q_ref, k_ref, v_ref, bias_ref, g_ref, o_ref, z_ref):
    nt, t = q_ref.shape[1], q_ref.shape[3]
    n_heads = q_ref.shape[2] // HEAD_DIM
    units = [(h, qi) for h in range(n_heads) for qi in range(nt)]
    ones_rows = jnp.ones((16, t), BF16)
    m_of, acc_of = {}, {}

    def head_rows(h):
        return slice(h * HEAD_DIM, (h + 1) * HEAD_DIM)

    def score_tile(u, kj):
        h, qi = units[u]
        z = lax.dot_general(k_ref[0, kj, head_rows(h), :], q_ref[0, qi, head_rows(h), :], _CONTRACT_ROWS,
                            preferred_element_type=F32)
        z = z + bias_ref[h, qi - kj]
        z_ref[u % 2, kj] = z
        zm = jnp.max(z, axis=0, keepdims=True)
        m_of[u] = zm if kj == 0 else jnp.maximum(m_of[u], zm)

    def attend_tile(u, kj):
        h, qi = units[u]
        p = jnp.exp2(z_ref[u % 2, kj] - m_of[u]).astype(BF16)
        v_ext = jnp.concatenate([v_ref[0, kj, head_rows(h), :], ones_rows], axis=0)
        pv = jnp.dot(v_ext, p, preferred_element_type=F32)
        acc = pv if kj == 0 else acc_of[u] + pv
        acc_of[u] = acc
        if kj == qi:
            out = acc[:HEAD_DIM] / acc[HEAD_DIM:HEAD_DIM + 1]
            o_ref[0, qi, head_rows(h), :] = _head_rmsnorm_t(out, g_ref[h]).astype(BF16)
            del acc_of[u], m_of[u]

    steps = [functools.partial(score_tile, 0, kj) for kj in range(units[0][1] + 1)]
    for u, (_, qi) in enumerate(units):
        if u + 1 < len(units):
            steps += [functools.partial(score_tile, u + 1, kj) for kj in range(units[u + 1][1] + 1)]
        steps += [functools.partial(attend_tile, u, kj) for kj in range(qi + 1)]
    return steps


def _sb_kernel(q_ref, k_ref, v_ref, u_ref, g_ref, o_ref):
    for step in _sb_program(q_ref, k_ref, v_ref, u_ref, g_ref, o_ref):
        step()


def _dil_kernel(q_ref, k_ref, v_ref, bias_ref, g_ref, o_ref, z_ref):
    steps, overflowed = _dil_fast_program(q_ref, k_ref, v_ref, bias_ref, g_ref, o_ref)
    for step in steps:
        step()

    @pl.when(overflowed() > 0)
    def _():
        for step in _dil_exact_program(q_ref, k_ref, v_ref, bias_ref, g_ref, o_ref, z_ref):
            step()


def _attention(qkv_t, bias_tiles, g_sb, g_dil, sb_blk, dil_blk):
    b, nt, _, t = qkv_t.shape
    hps = HEADS_PER_STEP
    blk = (1, nt, hps * HEAD_DIM, t)

    def rows(first_block):
        return pl.BlockSpec(blk, lambda h, i: (i, 0, first_block // hps + h, 0))

    def call(body, name, g_out, row_blocks, extra_specs, extra_args, scratch_shapes=()):
        nh = g_out.shape[0]
        return pl.pallas_call(
            body,
            grid=(nh // hps, b),
            in_specs=[rows(r) for r in row_blocks] + extra_specs + [
                pl.BlockSpec((hps, HEAD_DIM, 1), lambda h, i: (h, 0, 0))],
            out_specs=pl.BlockSpec(blk, lambda h, i: (i, 0, h, 0)),
            out_shape=jax.ShapeDtypeStruct((b, nt, nh * HEAD_DIM, t), BF16),
            scratch_shapes=list(scratch_shapes),
            compiler_params=_cparams(2),
            name=name,
        )(*([qkv_t] * 3), *extra_args, g_out.reshape(nh, HEAD_DIM, 1))

    o_sb = call(_sb_kernel, "stickbreak_attn", g_sb, sb_blk,
                [_resident((SB_SUB_TILE, 2 * SB_SUB_TILE), lambda h, i: (0, 0))], [_suffix_matrix(SB_SUB_TILE)])
    o_dil = call(_dil_kernel, "dilated_attn", g_dil, dil_blk,
                 [pl.BlockSpec((hps, nt, t, t), lambda h, i: (h, 0, 0, 0))], [bias_tiles],
                 scratch_shapes=[pltpu.VMEM((2, nt, t, t), F32)])
    return o_sb, o_dil


def kernel(x, c, w_ada, b_ada, g_ffn1, w1_gate, w1_up, w1_down, g_mix, w_in, g_sb_out, g_dil_out, w_out, rel_bias, g_ffn2, w2_gate, w2_up, w2_down, g_final):
    depth = w_ada.shape[0]
    seq = x.shape[1]
    nh_sb = g_sb_out.shape[1]
    nh_dil = g_dil_out.shape[1]
    d_sb = nh_sb * HEAD_DIM
    sb_blk = (0, nh_sb, 2 * nh_sb)
    dil_blk = (3 * nh_sb, 3 * nh_sb + nh_dil, 3 * nh_sb + 2 * nh_dil)
    d_dil = nh_dil * HEAD_DIM
    assert d_sb % PROJ_ROW_CHUNK == 0 and d_dil % PROJ_ROW_CHUNK == 0
    q_chunks = tuple(range(d_sb // PROJ_ROW_CHUNK)) + tuple(
        range(3 * d_sb // PROJ_ROW_CHUNK, (3 * d_sb + d_dil) // PROJ_ROW_CHUNK))
    bias_tiles = _dilated_bias_tiles(rel_bias, seq, ATT_TILE)
    for l in range(depth):
        mod = _adaln_mod(c, w_ada[l], b_ada[l])
        x, qkv_t = _ffn(x, mod, g_ffn1[l], w1_gate[l].astype(BF16), w1_up[l].astype(BF16), w1_down[l].astype(BF16), 0,
                        proj=(g_mix[l], w_in[l].T.astype(BF16), 3, q_chunks))
        o_sb, o_dil = _attention(qkv_t, bias_tiles, g_sb_out[l], g_dil_out[l], sb_blk, dil_blk)
        w_o = w_out[l].astype(BF16)
        last = l == depth - 1
        x = _ffn(x, mod, g_ffn2[l], w2_gate[l].astype(BF16), w2_up[l].astype(BF16), w2_down[l].astype(BF16), 6,
                 mixer=(o_sb, o_dil, w_o[:d_sb], w_o[d_sb:], 3), g_final=g_final if last else None)
    return x
```

```python
import functools
import math

import jax
import jax.numpy as jnp
import numpy as np
from jax import lax
from jax.experimental import pallas as pl
from jax.experimental.pallas import tpu as pltpu

F32 = jnp.float32
BF16 = jnp.bfloat16

HEAD_DIM = 64
DIL_CONFIGS = ((128, 1), (512, 4), (2048, 16))
MAX_DISTANCE = 2048
N_MOD = 9
EPS = 1e-6
NEG_INF = -1e30

ATT_TILE = 256
SB_SUB_TILE = 128
HEADS_PER_STEP = 2
SOFTPLUS_LINEAR_FROM = 64.0
SB_STAGE_LEADS = (1, 1)
DIL_SCORE_LEAD = 4
LOG2E = 1.4426950408889634
FFN_TOKEN_TILE = 512
PROJ_ROW_CHUNK = 512
VMEM_LIMIT = 56 * 1024 * 1024


def _cparams(n_grid):
    return pltpu.CompilerParams(dimension_semantics=("arbitrary",) * n_grid, vmem_limit_bytes=VMEM_LIMIT)


def _resident(block_shape, index_map):
    return pl.BlockSpec(block_shape, index_map, pipeline_mode=pl.Buffered(1))


def _mod_kernel(c_ref, w_ref, b_ref, o_ref):
    c = c_ref[...]
    s = c * jax.nn.sigmoid(c)
    o_ref[...] = jnp.dot(s.astype(BF16), w_ref[...].astype(BF16), preferred_element_type=F32) + b_ref[...]


def _adaln_mod(c, w_ada, b_ada):
    b, d = c.shape
    n = w_ada.shape[1]
    tn = 1024
    out = pl.pallas_call(
        _mod_kernel,
        grid=(n // tn,),
        in_specs=[
            pl.BlockSpec((b, d), lambda j: (0, 0)),
            pl.BlockSpec((d, tn), lambda j: (0, j)),
            pl.BlockSpec((1, tn), lambda j: (0, j)),
        ],
        out_specs=pl.BlockSpec((b, tn), lambda j: (0, j)),
        out_shape=jax.ShapeDtypeStruct((b, n), F32),
        compiler_params=_cparams(1),
        name="adaln_mod",
    )(c, w_ada, b_ada.reshape(1, n))
    return out.reshape(b, N_MOD, d)


def _modulate(x, g, shift, scale):
    y = x * lax.rsqrt(jnp.mean(x * x, axis=-1, keepdims=True) + EPS)
    return (y * g) * (1.0 + scale) + shift


def _ffn_kernel(*refs, mod_base, mixer_base, proj_base, q_chunks, final):
    x_ref, mod_ref, g_ref, wg_ref, wu_ref, wd_ref = refs[:6]
    rest = list(refs[6:])
    qkv_ref = rest.pop() if proj_base is not None else None
    o_ref = rest.pop()
    gf_ref = rest.pop() if final else None
    gmix_ref, win_ref = (rest.pop(-2), rest.pop()) if proj_base is not None else (None, None)
    mixer_refs = rest if mixer_base is not None else None
    shift = mod_ref[0, mod_base:mod_base + 1, :]
    scale = mod_ref[0, mod_base + 1:mod_base + 2, :]
    gate = mod_ref[0, mod_base + 2:mod_base + 3, :]
    sub = ATT_TILE
    n_sub = x_ref.shape[1] // sub
    xs, hs, acts = {}, {}, {}

    def prologue(i):
        x = x_ref[0, i * sub:(i + 1) * sub, :]
        if mixer_refs is not None:
            osb_ref, odil_ref, wsb_ref, wdil_ref = mixer_refs
            y = lax.dot_general(osb_ref[0, i], wsb_ref[...], _CONTRACT_ROWS, preferred_element_type=F32)
            y = y + lax.dot_general(odil_ref[0, i], wdil_ref[...], _CONTRACT_ROWS, preferred_element_type=F32)
            x = x + mod_ref[0, mixer_base + 2:mixer_base + 3, :] * y
        xs[i] = x
        hs[i] = _modulate(x, g_ref[...], shift, scale).astype(BF16)

    def expand(i):
        hb = hs.pop(i)
        gg = jnp.dot(hb, wg_ref[...], preferred_element_type=F32)
        uu = jnp.dot(hb, wu_ref[...], preferred_element_type=F32)
        acts[i] = ((gg * jax.nn.sigmoid(gg)) * uu).astype(BF16)

    def contract(i):
        out = xs.pop(i) + (0.5 * gate) * jnp.dot(acts.pop(i), wd_ref[...], preferred_element_type=F32)
        if proj_base is not None:
            xs[i] = out
        if final:
            out = out * lax.rsqrt(jnp.mean(out * out, axis=-1, keepdims=True) + EPS) * gf_ref[...]
        o_ref[0, i * sub:(i + 1) * sub, :] = out

    def project(i):
        hb = _modulate(xs.pop(i), gmix_ref[...], mod_ref[0, proj_base:proj_base + 1, :],
                       mod_ref[0, proj_base + 1:proj_base + 2, :]).astype(BF16)
        q_scale = HEAD_DIM ** -0.5 * LOG2E
        for r in range(win_ref.shape[0] // PROJ_ROW_CHUNK):
            rows = slice(r * PROJ_ROW_CHUNK, (r + 1) * PROJ_ROW_CHUNK)
            res = lax.dot_general(win_ref[rows, :], hb, (((1,), (1,)), ((), ())), preferred_element_type=F32)
            if r in q_chunks:
                res = res * q_scale
            qkv_ref[0, i, rows, :] = res.astype(BF16)

    for i in range(n_sub):
        prologue(i)
    for i in range(n_sub):
        expand(i)
    for i in range(n_sub):
        contract(i)
    if proj_base is not None:
        for i in range(n_sub):
            project(i)


def _ffn(x, mod, g, wg, wu, wd, mod_base, mixer=None, proj=None, g_final=None):
    b, s, d = x.shape
    dff = wg.shape[1]
    tm = FFN_TOKEN_TILE
    final = g_final is not None
    in_specs = [
        pl.BlockSpec((1, tm, d), lambda i, j: (i, j, 0)),
        pl.BlockSpec((1, N_MOD, d), lambda i, j: (i, 0, 0)),
        _resident((1, d), lambda i, j: (0, 0)),
        _resident((d, dff), lambda i, j: (0, 0)),
        _resident((d, dff), lambda i, j: (0, 0)),
        _resident((dff, d), lambda i, j: (0, 0)),
    ]
    args = [x, mod, g.reshape(1, d), wg, wu, wd]
    mixer_base = None
    if mixer is not None:
        o_sb, o_dil, w_sb, w_dil, mixer_base = mixer
        t = o_sb.shape[3]
        for o_heads in (o_sb, o_dil):
            in_specs.append(pl.BlockSpec((1, tm // t, o_heads.shape[2], t), lambda i, j: (i, j, 0, 0)))
            args.append(o_heads)
        for w_heads in (w_sb, w_dil):
            in_specs.append(_resident(w_heads.shape, lambda i, j: (0, 0)))
            args.append(w_heads)
    out_specs = pl.BlockSpec((1, tm, d), lambda i, j: (i, j, 0))
    out_shape = jax.ShapeDtypeStruct((b, s, d), F32)
    proj_base, q_chunks = None, ()
    if proj is not None:
        g_mix, w_in_t, proj_base, q_chunks = proj
        n = w_in_t.shape[0]
        in_specs += [_resident((1, d), lambda i, j: (0, 0)), _resident((n, d), lambda i, j: (0, 0))]
        args += [g_mix.reshape(1, d), w_in_t]
        out_specs = (out_specs, pl.BlockSpec((1, tm // ATT_TILE, n, ATT_TILE), lambda i, j: (i, j, 0, 0)))
        out_shape = (out_shape, jax.ShapeDtypeStruct((b, s // ATT_TILE, n, ATT_TILE), BF16))
    if final:
        in_specs.append(_resident((1, d), lambda i, j: (0, 0)))
        args.append(g_final.reshape(1, d))
    return pl.pallas_call(
        functools.partial(_ffn_kernel, mod_base=mod_base, mixer_base=mixer_base, proj_base=proj_base,
                          q_chunks=q_chunks, final=final),
        grid=(b, s // tm),
        in_specs=in_specs,
        out_specs=out_specs,
        out_shape=out_shape,
        compiler_params=_cparams(2),
        name="ffn_mixer_out" if mixer is not None else "ffn_qkv" if proj is not None else "ffn",
    )(*args)


def _head_rmsnorm_t(o, g_col):
    return (o * lax.rsqrt(jnp.mean(o * o, axis=0, keepdims=True) + EPS)) * g_col


_CONTRACT_ROWS = (((0,), (0,)), ((), ()))


def _sb_program(q_ref, k_ref, v_ref, u_ref, g_ref, o_ref):
    nt, t = q_ref.shape[1], q_ref.shape[3]
    n_heads = q_ref.shape[2] // HEAD_DIM
    ks = u_ref.shape[0]
    sub = t // ks
    ucat = u_ref[...]
    tiles = [(h, qi, kj) for h in range(n_heads) for qi in range(nt) for kj in range(qi, -1, -1)]
    subs = range(sub - 1, -1, -1)
    z_of, cat_of, incl_of, w_of = {}, {}, {}, {}
    state = {}

    def head_rows(h):
        return slice(h * HEAD_DIM, (h + 1) * HEAD_DIM)

    def first_lane(qi, kj, si):
        return si * ks if kj == qi else 0

    def causal(lanes):
        return lax.broadcasted_iota(jnp.int32, (ks, lanes), 0) < lax.broadcasted_iota(jnp.int32, (ks, lanes), 1)

    def scores(n):
        h, qi, kj = tiles[n]
        q_t = q_ref[0, qi, head_rows(h), :]
        z_of[n] = [lax.dot_general(k_ref[0, kj, head_rows(h), si * ks:(si + 1) * ks],
                                   q_t[:, first_lane(qi, kj, si):], _CONTRACT_ROWS,
                                   preferred_element_type=F32) for si in range(sub)]

    def softplus(n):
        h, qi, kj = tiles[n]
        cats = []
        for z in z_of[n]:
            sp = jnp.maximum(z, jnp.log(1.0 + jnp.exp2(jnp.minimum(z, SOFTPLUS_LINEAR_FROM))) * LOG2E)
            if kj == qi:
                sp = jnp.where(causal(z.shape[1]), sp, 0.0)
            hi = sp.astype(BF16)
            lo = (sp - hi.astype(F32)).astype(BF16)
            cats.append(jnp.concatenate([hi, lo], axis=0))
        cat_of[n] = cats

    def cumsum(n):
        incl_of[n] = [jnp.dot(ucat, c, preferred_element_type=F32) for c in cat_of.pop(n)]

    def weights(n):
        h, qi, kj = tiles[n]
        carry = state.setdefault((h, qi), [None, None])[0]
        zs, incls = z_of.pop(n), incl_of.pop(n)
        parts = [None] * sub
        for si in subs:
            skipped = first_lane(qi, kj, si)
            arg = zs[si] - incls[si]
            if carry is not None:
                arg = arg - carry[:, skipped:]
            w = jnp.exp2(arg)
            total = incls[si][0:1, :]
            if kj == qi:
                w = jnp.where(causal(w.shape[1]), w, 0.0)
            if skipped:
                w = jnp.concatenate([jnp.zeros((ks, skipped), F32), w], axis=1)
                total = jnp.concatenate([jnp.zeros((1, skipped), F32), total], axis=1)
            parts[si] = w.astype(BF16)
            carry = total if carry is None else carry + total
        state[h, qi][0] = carry
        w_of[n] = jnp.concatenate(parts, axis=0)

    def values(n):
        h, qi, kj = tiles[n]
        pv = jnp.dot(v_ref[0, kj, head_rows(h), :], w_of.pop(n), preferred_element_type=F32)
        acc = state[h, qi][1]
        acc = pv if acc is None else acc + pv
        state[h, qi][1] = acc
        if kj == 0:
            o_ref[0, qi, head_rows(h), :] = _head_rmsnorm_t(acc, g_ref[h]).astype(BF16)
            del state[h, qi]

    n_tiles = len(tiles)

    lead_a, lead_b = SB_STAGE_LEADS

    def step(s):
        if s < n_tiles:
            scores(s)
        if 0 <= s - lead_a < n_tiles:
            softplus(s - lead_a)
            cumsum(s - lead_a)
        if 0 <= s - lead_a - lead_b < n_tiles:
            weights(s - lead_a - lead_b)
            values(s - lead_a - lead_b)

    return [functools.partial(step, s) for s in range(n_tiles + lead_a + lead_b)]


def _suffix_matrix(n):
    u = (np.arange(n)[None, :] >= np.arange(n)[:, None]).astype(np.float32)
    return jnp.asarray(np.concatenate([u, u], axis=1), dtype=BF16)


def _t5_causal_bucket(n, n_buckets):
    max_exact = n_buckets // 2
    nf = np.maximum(n, 1).astype(np.float32)
    large = max_exact + (np.log(nf / max_exact) / math.log(MAX_DISTANCE / max_exact)
                         * (n_buckets - max_exact)).astype(np.int32)
    large = np.minimum(large, n_buckets - 1)
    return np.where(n < max_exact, n, large).astype(np.int32)


def _dilated_bias_tiles(rel_bias, seq, t):
    dist = np.arange(seq)
    mult = np.zeros(seq, np.float32)
    for window, dilation in DIL_CONFIGS:
        mult += ((dist % dilation == 0) & (dist <= window)).astype(np.float32)
    bucket = _t5_causal_bucket(dist, rel_bias.shape[0])
    per_dist = (rel_bias[bucket].astype(F32) + jnp.log(jnp.maximum(jnp.asarray(mult), 1.0))[:, None]) * LOG2E
    per_dist = jnp.where(jnp.asarray(mult > 0)[:, None], per_dist, NEG_INF)
    nh = rel_bias.shape[1]
    table = jnp.concatenate([jnp.full((nh, t), NEG_INF, F32), per_dist.T], axis=1).reshape(nh, 1, seq + t)
    return pl.pallas_call(
        _bias_tile_kernel,
        grid=(nh,),
        in_specs=[pl.BlockSpec((1, 1, seq + t), lambda h: (h, 0, 0))],
        out_specs=pl.BlockSpec((1, seq // t, t, t), lambda h: (h, 0, 0, 0)),
        out_shape=jax.ShapeDtypeStruct((nh, seq // t, t, t), F32),
        compiler_params=_cparams(1),
        name="dilated_bias_tiles",
    )(table)


def _bias_tile_kernel(tab_ref, o_ref):
    no, t = o_ref.shape[1], o_ref.shape[2]
    for o in range(no):
        window = jnp.broadcast_to(tab_ref[0, :, o * t:(o + 2) * t], (t, 2 * t))
        skewed = pltpu.roll(window, 0, 1, stride=1, stride_axis=0)
        o_ref[0, o] = skewed[:, t:]


def _dil_fast_program(q_ref, k_ref, v_ref, bias_ref, g_ref, o_ref):
    nt, t = q_ref.shape[1], q_ref.shape[3]
    n_heads = q_ref.shape[2] // HEAD_DIM
    ones_rows = jnp.ones((16, t), BF16)
    tiles = [(h, qi, kj) for h in range(n_heads) for qi in range(nt) for kj in range(qi, -1, -1)]
    z_of, state = {}, {}
    bad = [jnp.zeros((HEAD_DIM + 16, t), jnp.int32)]

    def head_rows(h):
        return slice(h * HEAD_DIM, (h + 1) * HEAD_DIM)

    def scores(n):
        h, qi, kj = tiles[n]
        z = lax.dot_general(k_ref[0, kj, head_rows(h), :], q_ref[0, qi, head_rows(h), :], _CONTRACT_ROWS,
                            preferred_element_type=F32)
        z_of[n] = z + bias_ref[h, qi - kj]

    def attend(n):
        h, qi, kj = tiles[n]
        z = z_of.pop(n)
        if kj == qi:
            state[h, qi] = [jnp.max(z, axis=0, keepdims=True), None]
        m, acc = state[h, qi]
        v_ext = jnp.concatenate([v_ref[0, kj, head_rows(h), :], ones_rows], axis=0)
        pv = jnp.dot(v_ext, jnp.exp2(z - m).astype(BF16), preferred_element_type=F32)
        acc = pv if acc is None else acc + pv
        state[h, qi][1] = acc
        if kj == 0:
            bad[0] = jnp.maximum(bad[0], jnp.where(jnp.isfinite(acc), 0, 1))
            out = acc[:HEAD_DIM] / acc[HEAD_DIM:HEAD_DIM + 1]
            o_ref[0, qi, head_rows(h), :] = _head_rmsnorm_t(out, g_ref[h]).astype(BF16)
            del state[h, qi]

    def step(s):
        if s < len(tiles):
            scores(s)
        if s >= DIL_SCORE_LEAD:
            attend(s - DIL_SCORE_LEAD)

    def overflowed():
        return jnp.max(bad[0])

    return [functools.partial(step, s) for s in range(len(tiles) + DIL_SCORE_LEAD)], overflowed


def _dil_exact_program(q_ref, k_ref, v_ref, bias_ref, g_ref, o_ref, z_ref):
    nt, t = q_ref.shape[1], q_ref.shape[3]
    n_heads = q_ref.shape[2] // HEAD_DIM
    units = [(h, qi) for h in range(n_heads) for qi in range(nt)]
    ones_rows = jnp.ones((16, t), BF16)
    m_of, acc_of = {}, {}

    def head_rows(h):
        return slice(h * HEAD_DIM, (h + 1) * HEAD_DIM)

    def score_tile(u, kj):
        h, qi = units[u]
        z = lax.dot_general(k_ref[0, kj, head_rows(h), :], q_ref[0, qi, head_rows(h), :], _CONTRACT_ROWS,
                            preferred_element_type=F32)
        z = z + bias_ref[h, qi - kj]
        z_ref[u % 2, kj] = z
        zm = jnp.max(z, axis=0, keepdims=True)
        m_of[u] = zm if kj == 0 else jnp.maximum(m_of[u], zm)

    def attend_tile(u, kj):
        h, qi = units[u]
        p = jnp.exp2(z_ref[u % 2, kj] - m_of[u]).astype(BF16)
        v_ext = jnp.concatenate([v_ref[0, kj, head_rows(h), :], ones_rows], axis=0)
        pv = jnp.dot(v_ext, p, preferred_element_type=F32)
        acc = pv if kj == 0 else acc_of[u] + pv
        acc_of[u] = acc
        if kj == qi:
            out = acc[:HEAD_DIM] / acc[HEAD_DIM:HEAD_DIM + 1]
            o_ref[0, qi, head_rows(h), :] = _head_rmsnorm_t(out, g_ref[h]).astype(BF16)
            del acc_of[u], m_of[u]

    steps = [functools.partial(score_tile, 0, kj) for kj in range(units[0][1] + 1)]
    for u, (_, qi) in enumerate(units):
        if u + 1 < len(units):
            steps += [functools.partial(score_tile, u + 1, kj) for kj in range(units[u + 1][1] + 1)]
        steps += [functools.partial(attend_tile, u, kj) for kj in range(qi + 1)]
    return steps


def _sb_kernel(q_ref, k_ref, v_ref, u_ref, g_ref, o_ref):
    for step in _sb_program(q_ref, k_ref, v_ref, u_ref, g_ref, o_ref):
        step()


def _dil_kernel(q_ref, k_ref, v_ref, bias_ref, g_ref, o_ref, z_ref):
    steps, overflowed = _dil_fast_program(q_ref, k_ref, v_ref, bias_ref, g_ref, o_ref)
    for step in steps:
        step()

    @pl.when(overflowed() > 0)
    def _():
        for step in _dil_exact_program(q_ref, k_ref, v_ref, bias_ref, g_ref, o_ref, z_ref):
            step()


def _attention(qkv_t, bias_tiles, g_sb, g_dil, sb_blk, dil_blk):
    b, nt, _, t = qkv_t.shape
    hps = HEADS_PER_STEP
    blk = (1, nt, hps * HEAD_DIM, t)

    def rows(first_block):
        return pl.BlockSpec(blk, lambda h, i: (i, 0, first_block // hps + h, 0))

    def call(body, name, g_out, row_blocks, extra_specs, extra_args, scratch_shapes=()):
        nh = g_out.shape[0]
        return pl.pallas_call(
            body,
            grid=(nh // hps, b),
            in_specs=[rows(r) for r in row_blocks] + extra_specs + [
                pl.BlockSpec((hps, HEAD_DIM, 1), lambda h, i: (h, 0, 0))],
            out_specs=pl.BlockSpec(blk, lambda h, i: (i, 0, h, 0)),
            out_shape=jax.ShapeDtypeStruct((b, nt, nh * HEAD_DIM, t), BF16),
            scratch_shapes=list(scratch_shapes),
            compiler_params=_cparams(2),
            name=name,
        )(*([qkv_t] * 3), *extra_args, g_out.reshape(nh, HEAD_DIM, 1))

    o_sb = call(_sb_kernel, "stickbreak_attn", g_sb, sb_blk,
                [_resident((SB_SUB_TILE, 2 * SB_SUB_TILE), lambda h, i: (0, 0))], [_suffix_matrix(SB_SUB_TILE)])
    o_dil = call(_dil_kernel, "dilated_attn", g_dil, dil_blk,
                 [pl.BlockSpec((hps, nt, t, t), lambda h, i: (h, 0, 0, 0))], [bias_tiles],
                 scratch_shapes=[pltpu.VMEM((2, nt, t, t), F32)])
    return o_sb, o_dil


def kernel(x, c, w_ada, b_ada, g_ffn1, w1_gate, w1_up, w1_down, g_mix, w_in, g_sb_out, g_dil_out, w_out, rel_bias, g_ffn2, w2_gate, w2_up, w2_down, g_final):
    depth = w_ada.shape[0]
    seq = x.shape[1]
    nh_sb = g_sb_out.shape[1]
    nh_dil = g_dil_out.shape[1]
    d_sb = nh_sb * HEAD_DIM
    sb_blk = (0, nh_sb, 2 * nh_sb)
    dil_blk = (3 * nh_sb, 3 * nh_sb + nh_dil, 3 * nh_sb + 2 * nh_dil)
    d_dil = nh_dil * HEAD_DIM
    assert d_sb % PROJ_ROW_CHUNK == 0 and d_dil % PROJ_ROW_CHUNK == 0
    q_chunks = tuple(range(d_sb // PROJ_ROW_CHUNK)) + tuple(
        range(3 * d_sb // PROJ_ROW_CHUNK, (3 * d_sb + d_dil) // PROJ_ROW_CHUNK))
    bias_tiles = _dilated_bias_tiles(rel_bias, seq, ATT_TILE)
    for l in range(depth):
        mod = _adaln_mod(c, w_ada[l], b_ada[l])
        x, qkv_t = _ffn(x, mod, g_ffn1[l], w1_gate[l].astype(BF16), w1_up[l].astype(BF16), w1_down[l].astype(BF16), 0,
                        proj=(g_mix[l], w_in[l].T.astype(BF16), 3, q_chunks))
        o_sb, o_dil = _attention(qkv_t, bias_tiles, g_sb_out[l], g_dil_out[l], sb_blk, dil_blk)
        w_o = w_out[l].astype(BF16)
        last = l == depth - 1
        x = _ffn(x, mod, g_ffn2[l], w2_gate[l].astype(BF16), w2_up[l].astype(BF16), w2_down[l].astype(BF16), 6,
                 mixer=(o_sb, o_dil, w_o[:d_sb], w_o[d_sb:], 3), g_final=g_final if last else None)
    return x
```

```python
import functools
import math

import jax
import jax.numpy as jnp
import numpy as np
from jax import lax
from jax.experimental import pallas as pl
from jax.experimental.pallas import tpu as pltpu

F32 = jnp.float32
BF16 = jnp.bfloat16

HEAD_DIM = 64
DIL_CONFIGS = ((128, 1), (512, 4), (2048, 16))
MAX_DISTANCE = 2048
N_MOD = 9
EPS = 1e-6
NEG_INF = -1e30

ATT_TILE = 256
SB_SUB_TILE = 128
HEADS_PER_STEP = 2
SOFTPLUS_LINEAR_FROM = 64.0
SB_STAGE_LEADS = (1, 1)
DIL_SCORE_LEAD = 5
LOG2E = 1.4426950408889634
FFN_TOKEN_TILE = 512
PROJ_ROW_CHUNK = 512
VMEM_LIMIT = 56 * 1024 * 1024


def _cparams(n_grid):
    return pltpu.CompilerParams(dimension_semantics=("arbitrary",) * n_grid, vmem_limit_bytes=VMEM_LIMIT)


def _resident(block_shape, index_map):
    return pl.BlockSpec(block_shape, index_map, pipeline_mode=pl.Buffered(1))


def _mod_kernel(c_ref, w_ref, b_ref, o_ref):
    c = c_ref[...]
    s = c * jax.nn.sigmoid(c)
    o_ref[...] = jnp.dot(s.astype(BF16), w_ref[...].astype(BF16), preferred_element_type=F32) + b_ref[...]


def _adaln_mod(c, w_ada, b_ada):
    b, d = c.shape
    n = w_ada.shape[1]
    tn = 1024
    out = pl.pallas_call(
        _mod_kernel,
        grid=(n // tn,),
        in_specs=[
            pl.BlockSpec((b, d), lambda j: (0, 0)),
            pl.BlockSpec((d, tn), lambda j: (0, j)),
            pl.BlockSpec((1, tn), lambda j: (0, j)),
        ],
        out_specs=pl.BlockSpec((b, tn), lambda j: (0, j)),
        out_shape=jax.ShapeDtypeStruct((b, n), F32),
        compiler_params=_cparams(1),
        name="adaln_mod",
    )(c, w_ada, b_ada.reshape(1, n))
    return out.reshape(b, N_MOD, d)


def _modulate(x, g, shift, scale):
    y = x * lax.rsqrt(jnp.mean(x * x, axis=-1, keepdims=True) + EPS)
    return (y * g) * (1.0 + scale) + shift


def _ffn_kernel(*refs, mod_base, mixer_base, proj_base, q_chunks, final):
    x_ref, mod_ref, g_ref, wg_ref, wu_ref, wd_ref = refs[:6]
    rest = list(refs[6:])
    qkv_ref = rest.pop() if proj_base is not None else None
    o_ref = rest.pop()
    gf_ref = rest.pop() if final else None
    gmix_ref, win_ref = (rest.pop(-2), rest.pop()) if proj_base is not None else (None, None)
    mixer_refs = rest if mixer_base is not None else None
    shift = mod_ref[0, mod_base:mod_base + 1, :]
    scale = mod_ref[0, mod_base + 1:mod_base + 2, :]
    gate = mod_ref[0, mod_base + 2:mod_base + 3, :]
    sub = ATT_TILE
    n_sub = x_ref.shape[1] // sub
    xs, hs, acts = {}, {}, {}

    def prologue(i):
        x = x_ref[0, i * sub:(i + 1) * sub, :]
        if mixer_refs is not None:
            osb_ref, odil_ref, wsb_ref, wdil_ref = mixer_refs
            y = lax.dot_general(osb_ref[0, i], wsb_ref[...], _CONTRACT_ROWS, preferred_element_type=F32)
            y = y + lax.dot_general(odil_ref[0, i], wdil_ref[...], _CONTRACT_ROWS, preferred_element_type=F32)
            x = x + mod_ref[0, mixer_base + 2:mixer_base + 3, :] * y
        xs[i] = x
        hs[i] = _modulate(x, g_ref[...], shift, scale).astype(BF16)

    def expand(i):
        hb = hs.pop(i)
        gg = jnp.dot(hb, wg_ref[...], preferred_element_type=F32)
        uu = jnp.dot(hb, wu_ref[...], preferred_element_type=F32)
        acts[i] = ((gg * jax.nn.sigmoid(gg)) * uu).astype(BF16)

    def contract(i):
        out = xs.pop(i) + (0.5 * gate) * jnp.dot(acts.pop(i), wd_ref[...], preferred_element_type=F32)
        if proj_base is not None:
            xs[i] = out
        if final:
            out = out * lax.rsqrt(jnp.mean(out * out, axis=-1, keepdims=True) + EPS) * gf_ref[...]
        o_ref[0, i * sub:(i + 1) * sub, :] = out

    def project(i):
        hb = _modulate(xs.pop(i), gmix_ref[...], mod_ref[0, proj_base:proj_base + 1, :],
                       mod_ref[0, proj_base + 1:proj_base + 2, :]).astype(BF16)
        q_scale = HEAD_DIM ** -0.5 * LOG2E
        for r in range(win_ref.shape[0] // PROJ_ROW_CHUNK):
            rows = slice(r * PROJ_ROW_CHUNK, (r + 1) * PROJ_ROW_CHUNK)
            res = lax.dot_general(win_ref[rows, :], hb, (((1,), (1,)), ((), ())), preferred_element_type=F32)
            if r in q_chunks:
                res = res * q_scale
            qkv_ref[0, i, rows, :] = res.astype(BF16)

    for i in range(n_sub):
        prologue(i)
    for i in range(n_sub):
        expand(i)
    for i in range(n_sub):
        contract(i)
    if proj_base is not None:
        for i in range(n_sub):
            project(i)


def _ffn(x, mod, g, wg, wu, wd, mod_base, mixer=None, proj=None, g_final=None):
    b, s, d = x.shape
    dff = wg.shape[1]
    tm = FFN_TOKEN_TILE
    final = g_final is not None
    in_specs = [
        pl.BlockSpec((1, tm, d), lambda i, j: (i, j, 0)),
        pl.BlockSpec((1, N_MOD, d), lambda i, j: (i, 0, 0)),
        _resident((1, d), lambda i, j: (0, 0)),
        _resident((d, dff), lambda i, j: (0, 0)),
        _resident((d, dff), lambda i, j: (0, 0)),
        _resident((dff, d), lambda i, j: (0, 0)),
    ]
    args = [x, mod, g.reshape(1, d), wg, wu, wd]
    mixer_base = None
    if mixer is not None:
        o_sb, o_dil, w_sb, w_dil, mixer_base = mixer
        t = o_sb.shape[3]
        for o_heads in (o_sb, o_dil):
            in_specs.append(pl.BlockSpec((1, tm // t, o_heads.shape[2], t), lambda i, j: (i, j, 0, 0)))
            args.append(o_heads)
        for w_heads in (w_sb, w_dil):
            in_specs.append(_resident(w_heads.shape, lambda i, j: (0, 0)))
            args.append(w_heads)
    out_specs = pl.BlockSpec((1, tm, d), lambda i, j: (i, j, 0))
    out_shape = jax.ShapeDtypeStruct((b, s, d), F32)
    proj_base, q_chunks = None, ()
    if proj is not None:
        g_mix, w_in_t, proj_base, q_chunks = proj
        n = w_in_t.shape[0]
        in_specs += [_resident((1, d), lambda i, j: (0, 0)), _resident((n, d), lambda i, j: (0, 0))]
        args += [g_mix.reshape(1, d), w_in_t]
        out_specs = (out_specs, pl.BlockSpec((1, tm // ATT_TILE, n, ATT_TILE), lambda i, j: (i, j, 0, 0)))
        out_shape = (out_shape, jax.ShapeDtypeStruct((b, s // ATT_TILE, n, ATT_TILE), BF16))
    if final:
        in_specs.append(_resident((1, d), lambda i, j: (0, 0)))
        args.append(g_final.reshape(1, d))
    return pl.pallas_call(
        functools.partial(_ffn_kernel, mod_base=mod_base, mixer_base=mixer_base, proj_base=proj_base,
                          q_chunks=q_chunks, final=final),
        grid=(b, s // tm),
        in_specs=in_specs,
        out_specs=out_specs,
        out_shape=out_shape,
        compiler_params=_cparams(2),
        name="ffn_mixer_out" if mixer is not None else "ffn_qkv" if proj is not None else "ffn",
    )(*args)


def _head_rmsnorm_t(o, g_col):
    return (o * lax.rsqrt(jnp.mean(o * o, axis=0, keepdims=True) + EPS)) * g_col


_CONTRACT_ROWS = (((0,), (0,)), ((), ()))


def _sb_program(q_ref, k_ref, v_ref, u_ref, g_ref, o_ref):
    nt, t = q_ref.shape[1], q_ref.shape[3]
    n_heads = q_ref.shape[2] // HEAD_DIM
    ks = u_ref.shape[0]
    sub = t // ks
    ucat = u_ref[...]
    tiles = [(h, qi, kj) for h in range(n_heads) for qi in range(nt) for kj in range(qi, -1, -1)]
    subs = range(sub - 1, -1, -1)
    z_of, cat_of, incl_of, w_of = {}, {}, {}, {}
    state = {}

    def head_rows(h):
        return slice(h * HEAD_DIM, (h + 1) * HEAD_DIM)

    def first_lane(qi, kj, si):
        return si * ks if kj == qi else 0

    def causal(lanes):
        return lax.broadcasted_iota(jnp.int32, (ks, lanes), 0) < lax.broadcasted_iota(jnp.int32, (ks, lanes), 1)

    def scores(n):
        h, qi, kj = tiles[n]
        q_t = q_ref[0, qi, head_rows(h), :]
        z_of[n] = [lax.dot_general(k_ref[0, kj, head_rows(h), si * ks:(si + 1) * ks],
                                   q_t[:, first_lane(qi, kj, si):], _CONTRACT_ROWS,
                                   preferred_element_type=F32) for si in range(sub)]

    def softplus(n):
        h, qi, kj = tiles[n]
        cats = []
        for z in z_of[n]:
            sp = jnp.maximum(z, jnp.log(1.0 + jnp.exp2(jnp.minimum(z, SOFTPLUS_LINEAR_FROM))) * LOG2E)
            if kj == qi:
                sp = jnp.where(causal(z.shape[1]), sp, 0.0)
            hi = sp.astype(BF16)
            lo = (sp - hi.astype(F32)).astype(BF16)
            cats.append(jnp.concatenate([hi, lo], axis=0))
        cat_of[n] = cats

    def cumsum(n):
        incl_of[n] = [jnp.dot(ucat, c, preferred_element_type=F32) for c in cat_of.pop(n)]

    def weights(n):
        h, qi, kj = tiles[n]
        carry = state.setdefault((h, qi), [None, None])[0]
        zs, incls = z_of.pop(n), incl_of.pop(n)
        parts = [None] * sub
        for si in subs:
            skipped = first_lane(qi, kj, si)
            arg = zs[si] - incls[si]
            if carry is not None:
                arg = arg - carry[:, skipped:]
            w = jnp.exp2(arg)
            total = incls[si][0:1, :]
            if kj == qi:
                w = jnp.where(causal(w.shape[1]), w, 0.0)
            if skipped:
                w = jnp.concatenate([jnp.zeros((ks, skipped), F32), w], axis=1)
                total = jnp.concatenate([jnp.zeros((1, skipped), F32), total], axis=1)
            parts[si] = w.astype(BF16)
            carry = total if carry is None else carry + total
        state[h, qi][0] = carry
        w_of[n] = jnp.concatenate(parts, axis=0)

    def values(n):
        h, qi, kj = tiles[n]
        pv = jnp.dot(v_ref[0, kj, head_rows(h), :], w_of.pop(n), preferred_element_type=F32)
        acc = state[h, qi][1]
        acc = pv if acc is None else acc + pv
        state[h, qi][1] = acc
        if kj == 0:
            o_ref[0, qi, head_rows(h), :] = _head_rmsnorm_t(acc, g_ref[h]).astype(BF16)
            del state[h, qi]

    n_tiles = len(tiles)

    lead_a, lead_b = SB_STAGE_LEADS

    def step(s):
        if s < n_tiles:
            scores(s)
        if 0 <= s - lead_a < n_tiles:
            softplus(s - lead_a)
            cumsum(s - lead_a)
        if 0 <= s - lead_a - lead_b < n_tiles:
            weights(s - lead_a - lead_b)
            values(s - lead_a - lead_b)

    return [functools.partial(step, s) for s in range(n_tiles + lead_a + lead_b)]


def _suffix_matrix(n):
    u = (np.arange(n)[None, :] >= np.arange(n)[:, None]).astype(np.float32)
    return jnp.asarray(np.concatenate([u, u], axis=1), dtype=BF16)


def _t5_causal_bucket(n, n_buckets):
    max_exact = n_buckets // 2
    nf = np.maximum(n, 1).astype(np.float32)
    large = max_exact + (np.log(nf / max_exact) / math.log(MAX_DISTANCE / max_exact)
                         * (n_buckets - max_exact)).astype(np.int32)
    large = np.minimum(large, n_buckets - 1)
    return np.where(n < max_exact, n, large).astype(np.int32)


def _dilated_bias_tiles(rel_bias, seq, t):
    dist = np.arange(seq)
    mult = np.zeros(seq, np.float32)
    for window, dilation in DIL_CONFIGS:
        mult += ((dist % dilation == 0) & (dist <= window)).astype(np.float32)
    bucket = _t5_causal_bucket(dist, rel_bias.shape[0])
    per_dist = (rel_bias[bucket].astype(F32) + jnp.log(jnp.maximum(jnp.asarray(mult), 1.0))[:, None]) * LOG2E
    per_dist = jnp.where(jnp.asarray(mult > 0)[:, None], per_dist, NEG_INF)
    nh = rel_bias.shape[1]
    table = jnp.concatenate([jnp.full((nh, t), NEG_INF, F32), per_dist.T], axis=1).reshape(nh, 1, seq + t)
    return pl.pallas_call(
        _bias_tile_kernel,
        grid=(nh,),
        in_specs=[pl.BlockSpec((1, 1, seq + t), lambda h: (h, 0, 0))],
        out_specs=pl.BlockSpec((1, seq // t, t, t), lambda h: (h, 0, 0, 0)),
        out_shape=jax.ShapeDtypeStruct((nh, seq // t, t, t), F32),
        compiler_params=_cparams(1),
        name="dilated_bias_tiles",
    )(table)


def _bias_tile_kernel(tab_ref, o_ref):
    no, t = o_ref.shape[1], o_ref.shape[2]
    for o in range(no):
        window = jnp.broadcast_to(tab_ref[0, :, o * t:(o + 2) * t], (t, 2 * t))
        skewed = pltpu.roll(window, 0, 1, stride=1, stride_axis=0)
        o_ref[0, o] = skewed[:, t:]


def _dil_fast_program(q_ref, k_ref, v_ref, bias_ref, g_ref, o_ref):
    nt, t = q_ref.shape[1], q_ref.shape[3]
    n_heads = q_ref.shape[2] // HEAD_DIM
    ones_rows = jnp.ones((16, t), BF16)
    tiles = [(h, qi, kj) for h in range(n_heads) for qi in range(nt) for kj in range(qi, -1, -1)]
    z_of, state = {}, {}
    bad = [jnp.zeros((HEAD_DIM + 16, t), jnp.int32)]

    def head_rows(h):
        return slice(h * HEAD_DIM, (h + 1) * HEAD_DIM)

    def scores(n):
        h, qi, kj = tiles[n]
        z = lax.dot_general(k_ref[0, kj, head_rows(h), :], q_ref[0, qi, head_rows(h), :], _CONTRACT_ROWS,
                            preferred_element_type=F32)
        z_of[n] = z + bias_ref[h, qi - kj]

    def attend(n):
        h, qi, kj = tiles[n]
        z = z_of.pop(n)
        if kj == qi:
            state[h, qi] = [jnp.max(z, axis=0, keepdims=True), None]
        m, acc = state[h, qi]
        v_ext = jnp.concatenate([v_ref[0, kj, head_rows(h), :], ones_rows], axis=0)
        pv = jnp.dot(v_ext, jnp.exp2(z - m).astype(BF16), preferred_element_type=F32)
        acc = pv if acc is None else acc + pv
        state[h, qi][1] = acc
        if kj == 0:
            bad[0] = jnp.maximum(bad[0], jnp.where(jnp.isfinite(acc), 0, 1))
            out = acc[:HEAD_DIM] / acc[HEAD_DIM:HEAD_DIM + 1]
            o_ref[0, qi, head_rows(h), :] = _head_rmsnorm_t(out, g_ref[h]).astype(BF16)
            del state[h, qi]

    def step(s):
        if s < len(tiles):
            scores(s)
        if s >= DIL_SCORE_LEAD:
            attend(s - DIL_SCORE_LEAD)

    def overflowed():
        return jnp.max(bad[0])

    return [functools.partial(step, s) for s in range(len(tiles) + DIL_SCORE_LEAD)], overflowed


def _dil_exact_program(q_ref, k_ref, v_ref, bias_ref, g_ref, o_ref, z_ref):
    nt, t = q_ref.shape[1], q_ref.shape[3]
    n_heads = q_ref.shape[2] // HEAD_DIM
    units = [(h, qi) for h in range(n_heads) for qi in range(nt)]
    ones_rows = jnp.ones((16, t), BF16)
    m_of, acc_of = {}, {}

    def head_rows(h):
        return slice(h * HEAD_DIM, (h + 1) * HEAD_DIM)

    def score_tile(u, kj):
        h, qi = units[u]
        z = lax.dot_general(k_ref[0, kj, head_rows(h), :], q_ref[0, qi, head_rows(h), :], _CONTRACT_ROWS,
                            preferred_element_type=F32)
        z = z + bias_ref[h, qi - kj]
        z_ref[u % 2, kj] = z
        zm = jnp.max(z, axis=0, keepdims=True)
        m_of[u] = zm if kj == 0 else jnp.maximum(m_of[u], zm)

    def attend_tile(u, kj):
        h, qi = units[u]
        p = jnp.exp2(z_ref[u % 2, kj] - m_of[u]).astype(BF16)
        v_ext = jnp.concatenate([v_ref[0, kj, head_rows(h), :], ones_rows], axis=0)
        pv = jnp.dot(v_ext, p, preferred_element_type=F32)
        acc = pv if kj == 0 else acc_of[u] + pv
        acc_of[u] = acc
        if kj == qi:
            out = acc[:HEAD_DIM] / acc[HEAD_DIM:HEAD_DIM + 1]
            o_ref[0, qi, head_rows(h), :] = _head_rmsnorm_t(out, g_ref[h]).astype(BF16)
            del acc_of[u], m_of[u]

    steps = [functools.partial(score_tile, 0, kj) for kj in range(units[0][1] + 1)]
    for u, (_, qi) in enumerate(units):
        if u + 1 < len(units):
            steps += [functools.partial(score_tile, u + 1, kj) for kj in range(units[u + 1][1] + 1)]
        steps += [functools.partial(attend_tile, u, kj) for kj in range(qi + 1)]
    return steps


def _attention_kernel(qa_ref, ka_ref, va_ref, qb_ref, kb_ref, vb_ref, u_ref, bias_ref, ga_ref, gb_ref,
                      oa_ref, ob_ref, z_ref):
    sb_steps = _sb_program(qa_ref, ka_ref, va_ref, u_ref, ga_ref, oa_ref)
    dil_steps, dil_overflowed = _dil_fast_program(qb_ref, kb_ref, vb_ref, bias_ref, gb_ref, ob_ref)
    n = max(len(sb_steps), len(dil_steps))
    done_sb = done_dil = 0
    for i in range(1, n + 1):
        for step in sb_steps[done_sb:len(sb_steps) * i // n]:
            step()
        for step in dil_steps[done_dil:len(dil_steps) * i // n]:
            step()
        done_sb, done_dil = len(sb_steps) * i // n, len(dil_steps) * i // n

    @pl.when(dil_overflowed() > 0)
    def _():
        for step in _dil_exact_program(qb_ref, kb_ref, vb_ref, bias_ref, gb_ref, ob_ref, z_ref):
            step()


def _attention(qkv_t, bias_tiles, g_sb, g_dil, sb_blk, dil_blk):
    b, nt, _, t = qkv_t.shape
    nh = g_sb.shape[0]
    assert g_dil.shape[0] == nh
    hps = HEADS_PER_STEP
    blk = (1, nt, hps * HEAD_DIM, t)

    def rows(first_block):
        return pl.BlockSpec(blk, lambda h, i: (i, 0, first_block // hps + h, 0))

    heads = pl.BlockSpec((hps, HEAD_DIM, 1), lambda h, i: (h, 0, 0))
    out_shape = jax.ShapeDtypeStruct((b, nt, nh * HEAD_DIM, t), BF16)
    return pl.pallas_call(
        _attention_kernel,
        grid=(nh // hps, b),
        in_specs=[rows(r) for r in sb_blk] + [rows(r) for r in dil_blk] + [
            _resident((SB_SUB_TILE, 2 * SB_SUB_TILE), lambda h, i: (0, 0)),
            pl.BlockSpec((hps, nt, t, t), lambda h, i: (h, 0, 0, 0)),
            heads, heads,
        ],
        out_specs=(pl.BlockSpec(blk, lambda h, i: (i, 0, h, 0)),) * 2,
        out_shape=(out_shape, out_shape),
        scratch_shapes=[pltpu.VMEM((2, nt, t, t), F32)],
        compiler_params=_cparams(2),
        name="attention",
    )(*([qkv_t] * 6), _suffix_matrix(SB_SUB_TILE), bias_tiles,
      g_sb.reshape(nh, HEAD_DIM, 1), g_dil.reshape(nh, HEAD_DIM, 1))


def kernel(x, c, w_ada, b_ada, g_ffn1, w1_gate, w1_up, w1_down, g_mix, w_in, g_sb_out, g_dil_out, w_out, rel_bias, g_ffn2, w2_gate, w2_up, w2_down, g_final):
    depth = w_ada.shape[0]
    seq = x.shape[1]
    nh_sb = g_sb_out.shape[1]
    nh_dil = g_dil_out.shape[1]
    d_sb = nh_sb * HEAD_DIM
    sb_blk = (0, nh_sb, 2 * nh_sb)
    dil_blk = (3 * nh_sb, 3 * nh_sb + nh_dil, 3 * nh_sb + 2 * nh_dil)
    d_dil = nh_dil * HEAD_DIM
    assert d_sb % PROJ_ROW_CHUNK == 0 and d_dil % PROJ_ROW_CHUNK == 0
    q_chunks = tuple(range(d_sb // PROJ_ROW_CHUNK)) + tuple(
        range(3 * d_sb // PROJ_ROW_CHUNK, (3 * d_sb + d_dil) // PROJ_ROW_CHUNK))
    bias_tiles = _dilated_bias_tiles(rel_bias, seq, ATT_TILE)
    for l in range(depth):
        mod = _adaln_mod(c, w_ada[l], b_ada[l])
        x, qkv_t = _ffn(x, mod, g_ffn1[l], w1_gate[l].astype(BF16), w1_up[l].astype(BF16), w1_down[l].astype(BF16), 0,
                        proj=(g_mix[l], w_in[l].T.astype(BF16), 3, q_chunks))
        o_sb, o_dil = _attention(qkv_t, bias_tiles, g_sb_out[l], g_dil_out[l], sb_blk, dil_blk)
        w_o = w_out[l].astype(BF16)
        last = l == depth - 1
        x = _ffn(x, mod, g_ffn2[l], w2_gate[l].astype(BF16), w2_up[l].astype(BF16), w2_down[l].astype(BF16), 6,
                 mixer=(o_sb, o_dil, w_o[:d_sb], w_o[d_sb:], 3), g_final=g_final if last else None)
    return x
```

```python
import functools
import math

import jax
import jax.numpy as jnp
import numpy as np
from jax import lax
from jax.experimental import pallas as pl
from jax.experimental.pallas import tpu as pltpu

F32 = jnp.float32
BF16 = jnp.bfloat16

HEAD_DIM = 64
DIL_CONFIGS = ((128, 1), (512, 4), (2048, 16))
MAX_DISTANCE = 2048
N_MOD = 9
EPS = 1e-6
NEG_INF = -1e30

ATT_TILE = 256
SB_SUB_TILE = 128
HEADS_PER_STEP = 2
SOFTPLUS_LINEAR_FROM = 64.0
SB_STAGE_LEADS = (1, 1)
DIL_SCORE_LEAD = 5
LOG2E = 1.4426950408889634
BF16_SUBLANES = 16
FFN_TOKEN_TILE = 512
FFN_OUT_TOKEN_TILE = 1024
PROJ_ROW_CHUNK = 512
MOD_COL_TILE = 2304
VMEM_LIMIT = 56 * 1024 * 1024


def _cparams(n_grid):
    return pltpu.CompilerParams(dimension_semantics=("arbitrary",) * n_grid, vmem_limit_bytes=VMEM_LIMIT)


def _resident(block_shape, index_map):
    return pl.BlockSpec(block_shape, index_map, pipeline_mode=pl.Buffered(1))


def _mod_kernel(c_ref, w_ref, b_ref, o_ref):
    c = c_ref[...]
    s = c * jax.nn.sigmoid(c)
    o_ref[...] = jnp.dot(s.astype(BF16), w_ref[...].astype(BF16), preferred_element_type=F32) + b_ref[...]


def _adaln_mod(c, w_ada, b_ada):
    b, d = c.shape
    n = w_ada.shape[1]
    tn = MOD_COL_TILE
    assert n % tn == 0 and n == N_MOD * d
    out = pl.pallas_call(
        _mod_kernel,
        grid=(n // tn,),
        in_specs=[
            pl.BlockSpec((b, d), lambda j: (0, 0)),
            pl.BlockSpec((d, tn), lambda j: (0, j)),
            pl.BlockSpec((1, tn), lambda j: (0, j)),
        ],
        out_specs=pl.BlockSpec((b, tn), lambda j: (0, j)),
        out_shape=jax.ShapeDtypeStruct((b, n), F32),
        compiler_params=_cparams(1),
        name="adaln_mod",
    )(c, w_ada, b_ada.reshape(1, n))
    return out.reshape(b, N_MOD, d)


def _modulate(x, g, shift, scale):
    y = x * lax.rsqrt(jnp.mean(x * x, axis=-1, keepdims=True) + EPS)
    return (y * g) * (1.0 + scale) + shift


def _ffn_kernel(*refs, mod_base, mixer_base, proj_base, q_chunks, final):
    x_ref, mod_ref, g_ref, wg_ref, wu_ref, wd_ref = refs[:6]
    rest = list(refs[6:])
    qkv_ref = rest.pop() if proj_base is not None else None
    o_ref = rest.pop()
    gf_ref = rest.pop() if final else None
    gmix_ref, win_ref = (rest.pop(-2), rest.pop()) if proj_base is not None else (None, None)
    mixer_refs = rest if mixer_base is not None else None
    shift = mod_ref[0, mod_base:mod_base + 1, :]
    scale = mod_ref[0, mod_base + 1:mod_base + 2, :]
    gate = mod_ref[0, mod_base + 2:mod_base + 3, :]
    sub = ATT_TILE
    n_sub = x_ref.shape[1] // sub
    xs, hs, acts = {}, {}, {}

    def prologue(i):
        x = x_ref[0, i * sub:(i + 1) * sub, :]
        if mixer_refs is not None:
            osb_ref, odil_ref, wsb_ref, wdil_ref = mixer_refs
            y = lax.dot_general(osb_ref[0, i], wsb_ref[...], _CONTRACT_ROWS, preferred_element_type=F32)
            y = y + lax.dot_general(odil_ref[0, i], wdil_ref[...], _CONTRACT_ROWS, preferred_element_type=F32)
            x = x + mod_ref[0, mixer_base + 2:mixer_base + 3, :] * y
        xs[i] = x
        hs[i] = _modulate(x, g_ref[...], shift, scale).astype(BF16)

    def expand(i):
        hb = hs.pop(i)
        gg = jnp.dot(hb, wg_ref[...], preferred_element_type=F32)
        uu = jnp.dot(hb, wu_ref[...], preferred_element_type=F32)
        acts[i] = ((gg * jax.nn.sigmoid(gg)) * uu).astype(BF16)

    def contract(i):
        out = xs.pop(i) + (0.5 * gate) * jnp.dot(acts.pop(i), wd_ref[...], preferred_element_type=F32)
        if proj_base is not None:
            xs[i] = out
        if final:
            out = out * lax.rsqrt(jnp.mean(out * out, axis=-1, keepdims=True) + EPS) * gf_ref[...]
        o_ref[0, i * sub:(i + 1) * sub, :] = out

    def project(i):
        hb = _modulate(xs.pop(i), gmix_ref[...], mod_ref[0, proj_base:proj_base + 1, :],
                       mod_ref[0, proj_base + 1:proj_base + 2, :]).astype(BF16)
        q_scale = HEAD_DIM ** -0.5 * LOG2E
        for r in range(win_ref.shape[0] // PROJ_ROW_CHUNK):
            rows = slice(r * PROJ_ROW_CHUNK, (r + 1) * PROJ_ROW_CHUNK)
            res = lax.dot_general(win_ref[rows, :], hb, (((1,), (1,)), ((), ())), preferred_element_type=F32)
            if r in q_chunks:
                res = res * q_scale
            qkv_ref[0, i, rows, :] = res.astype(BF16)

    for i in range(n_sub):
        prologue(i)
    for i in range(n_sub):
        expand(i)
    for i in range(n_sub):
        contract(i)
    if proj_base is not None:
        for i in range(n_sub):
            project(i)


def _ffn(x, mod, g, wg, wu, wd, mod_base, mixer=None, proj=None, g_final=None):
    b, s, d = x.shape
    dff = wg.shape[1]
    tm = FFN_TOKEN_TILE if proj is not None else FFN_OUT_TOKEN_TILE
    assert s % tm == 0 and tm % ATT_TILE == 0
    final = g_final is not None
    in_specs = [
        pl.BlockSpec((1, tm, d), lambda i, j: (i, j, 0)),
        pl.BlockSpec((1, N_MOD, d), lambda i, j: (i, 0, 0)),
        _resident((1, d), lambda i, j: (0, 0)),
        _resident((d, dff), lambda i, j: (0, 0)),
        _resident((d, dff), lambda i, j: (0, 0)),
        _resident((dff, d), lambda i, j: (0, 0)),
    ]
    args = [x, mod, g.reshape(1, d), wg, wu, wd]
    mixer_base = None
    if mixer is not None:
        o_sb, o_dil, w_sb, w_dil, mixer_base = mixer
        t = o_sb.shape[3]
        for o_heads in (o_sb, o_dil):
            in_specs.append(pl.BlockSpec((1, tm // t, o_heads.shape[2], t), lambda i, j: (i, j, 0, 0)))
            args.append(o_heads)
        for w_heads in (w_sb, w_dil):
            in_specs.append(_resident(w_heads.shape, lambda i, j: (0, 0)))
            args.append(w_heads)
    out_specs = pl.BlockSpec((1, tm, d), lambda i, j: (i, j, 0))
    out_shape = jax.ShapeDtypeStruct((b, s, d), F32)
    proj_base, q_chunks = None, ()
    if proj is not None:
        g_mix, w_in_t, proj_base, q_chunks = proj
        n = w_in_t.shape[0]
        in_specs += [_resident((1, d), lambda i, j: (0, 0)), _resident((n, d), lambda i, j: (0, 0))]
        args += [g_mix.reshape(1, d), w_in_t]
        out_specs = (out_specs, pl.BlockSpec((1, tm // ATT_TILE, n, ATT_TILE), lambda i, j: (i, j, 0, 0)))
        out_shape = (out_shape, jax.ShapeDtypeStruct((b, s // ATT_TILE, n, ATT_TILE), BF16))
    if final:
        in_specs.append(_resident((1, d), lambda i, j: (0, 0)))
        args.append(g_final.reshape(1, d))
    return pl.pallas_call(
        functools.partial(_ffn_kernel, mod_base=mod_base, mixer_base=mixer_base, proj_base=proj_base,
                          q_chunks=q_chunks, final=final),
        grid=(b, s // tm),
        in_specs=in_specs,
        out_specs=out_specs,
        out_shape=out_shape,
        compiler_params=_cparams(2),
        name="ffn_mixer_out" if mixer is not None else "ffn_qkv" if proj is not None else "ffn",
    )(*args)


def _head_rmsnorm_t(o, g_col):
    return (o * lax.rsqrt(jnp.mean(o * o, axis=0, keepdims=True) + EPS)) * g_col


_CONTRACT_ROWS = (((0,), (0,)), ((), ()))


def _sb_program(q_ref, k_ref, v_ref, u_ref, g_ref, o_ref):
    nt, t = q_ref.shape[1], q_ref.shape[3]
    n_heads = q_ref.shape[2] // HEAD_DIM
    ks = u_ref.shape[0]
    sub = t // ks
    ucat = u_ref[...]
    tiles = [(h, qi, kj) for h in range(n_heads) for qi in range(nt) for kj in range(qi, -1, -1)]
    subs = range(sub - 1, -1, -1)
    z_of, cat_of, incl_of, w_of = {}, {}, {}, {}
    state = {}

    def head_rows(h):
        return slice(h * HEAD_DIM, (h + 1) * HEAD_DIM)

    def first_lane(qi, kj, si):
        return si * ks if kj == qi else 0

    def causal(lanes):
        return lax.broadcasted_iota(jnp.int32, (ks, lanes), 0) < lax.broadcasted_iota(jnp.int32, (ks, lanes), 1)

    def scores(n):
        h, qi, kj = tiles[n]
        q_t = q_ref[0, qi, head_rows(h), :]
        z_of[n] = [lax.dot_general(k_ref[0, kj, head_rows(h), si * ks:(si + 1) * ks],
                                   q_t[:, first_lane(qi, kj, si):], _CONTRACT_ROWS,
                                   preferred_element_type=F32) for si in range(sub)]

    def softplus(n):
        h, qi, kj = tiles[n]
        cats = []
        for z in z_of[n]:
            sp = jnp.maximum(z, jnp.log(1.0 + jnp.exp2(jnp.minimum(z, SOFTPLUS_LINEAR_FROM))) * LOG2E)
            if kj == qi:
                sp = jnp.where(causal(z.shape[1]), sp, 0.0)
            hi = sp.astype(BF16)
            lo = (sp - hi.astype(F32)).astype(BF16)
            cats.append(jnp.concatenate([hi, lo], axis=0))
        cat_of[n] = cats

    def cumsum(n):
        incl_of[n] = [jnp.dot(ucat, c, preferred_element_type=F32) for c in cat_of.pop(n)]

    def weights(n):
        h, qi, kj = tiles[n]
        carry = state.setdefault((h, qi), [None, None])[0]
        zs, incls = z_of.pop(n), incl_of.pop(n)
        parts = [None] * sub
        for si in subs:
            skipped = first_lane(qi, kj, si)
            arg = zs[si] - incls[si]
            if carry is not None:
                arg = arg - carry[:, skipped:]
            w = jnp.exp2(arg)
            total = incls[si][0:1, :]
            if kj == qi:
                w = jnp.where(causal(w.shape[1]), w, 0.0)
            if skipped:
                w = jnp.concatenate([jnp.zeros((ks, skipped), F32), w], axis=1)
                total = jnp.concatenate([jnp.zeros((1, skipped), F32), total], axis=1)
            parts[si] = w.astype(BF16)
            carry = total if carry is None else carry + total
        state[h, qi][0] = carry
        w_of[n] = jnp.concatenate(parts, axis=0)

    def values(n):
        h, qi, kj = tiles[n]
        pv = jnp.dot(v_ref[0, kj, head_rows(h), :], w_of.pop(n), preferred_element_type=F32)
        acc = state[h, qi][1]
        acc = pv if acc is None else acc + pv
        state[h, qi][1] = acc
        if kj == 0:
            o_ref[0, qi, head_rows(h), :] = _head_rmsnorm_t(acc, g_ref[h]).astype(BF16)
            del state[h, qi]

    n_tiles = len(tiles)

    lead_a, lead_b = SB_STAGE_LEADS

    def step(s):
        if s < n_tiles:
            scores(s)
        if 0 <= s - lead_a < n_tiles:
            softplus(s - lead_a)
            cumsum(s - lead_a)
        if 0 <= s - lead_a - lead_b < n_tiles:
            weights(s - lead_a - lead_b)
            values(s - lead_a - lead_b)

    return [functools.partial(step, s) for s in range(n_tiles + lead_a + lead_b)]


def _suffix_matrix(n):
    u = (np.arange(n)[None, :] >= np.arange(n)[:, None]).astype(np.float32)
    return jnp.asarray(np.concatenate([u, u], axis=1), dtype=BF16)


def _t5_causal_bucket(n, n_buckets):
    max_exact = n_buckets // 2
    nf = np.maximum(n, 1).astype(np.float32)
    large = max_exact + (np.log(nf / max_exact) / math.log(MAX_DISTANCE / max_exact)
                         * (n_buckets - max_exact)).astype(np.int32)
    large = np.minimum(large, n_buckets - 1)
    return np.where(n < max_exact, n, large).astype(np.int32)


def _dilated_bias_tiles(rel_bias, seq, t):
    dist = np.arange(seq)
    mult = np.zeros(seq, np.float32)
    for window, dilation in DIL_CONFIGS:
        mult += ((dist % dilation == 0) & (dist <= window)).astype(np.float32)
    bucket = _t5_causal_bucket(dist, rel_bias.shape[0])
    per_dist = (rel_bias[bucket].astype(F32) + jnp.log(jnp.maximum(jnp.asarray(mult), 1.0))[:, None]) * LOG2E
    per_dist = jnp.where(jnp.asarray(mult > 0)[:, None], per_dist, NEG_INF)
    nh = rel_bias.shape[1]
    table = jnp.concatenate([jnp.full((nh, t), NEG_INF, F32), per_dist.T], axis=1).reshape(nh, 1, seq + t)
    return pl.pallas_call(
        _bias_tile_kernel,
        grid=(nh,),
        in_specs=[pl.BlockSpec((1, 1, seq + t), lambda h: (h, 0, 0))],
        out_specs=pl.BlockSpec((1, seq // t, t, t), lambda h: (h, 0, 0, 0)),
        out_shape=jax.ShapeDtypeStruct((nh, seq // t, t, t), F32),
        compiler_params=_cparams(1),
        name="dilated_bias_tiles",
    )(table)


def _bias_tile_kernel(tab_ref, o_ref):
    no, t = o_ref.shape[1], o_ref.shape[2]
    for o in range(no):
        window = jnp.broadcast_to(tab_ref[0, :, o * t:(o + 2) * t], (t, 2 * t))
        skewed = pltpu.roll(window, 0, 1, stride=1, stride_axis=0)
        o_ref[0, o] = skewed[:, t:]


def _dil_fast_program(q_ref, k_ref, v_ref, bias_ref, g_ref, o_ref):
    nt, t = q_ref.shape[1], q_ref.shape[3]
    n_heads = q_ref.shape[2] // HEAD_DIM
    ones_rows = jnp.ones((BF16_SUBLANES, t), BF16)
    tiles = [(h, qi, kj) for h in range(n_heads) for qi in range(nt) for kj in range(qi, -1, -1)]
    z_of, state = {}, {}
    bad = [jnp.zeros((HEAD_DIM + BF16_SUBLANES, t), jnp.int32)]

    def head_rows(h):
        return slice(h * HEAD_DIM, (h + 1) * HEAD_DIM)

    def scores(n):
        h, qi, kj = tiles[n]
        z = lax.dot_general(k_ref[0, kj, head_rows(h), :], q_ref[0, qi, head_rows(h), :], _CONTRACT_ROWS,
                            preferred_element_type=F32)
        z_of[n] = z + bias_ref[h, qi - kj]

    def attend(n):
        h, qi, kj = tiles[n]
        z = z_of.pop(n)
        if kj == qi:
            state[h, qi] = [jnp.max(z, axis=0, keepdims=True), None]
        m, acc = state[h, qi]
        v_ext = jnp.concatenate([v_ref[0, kj, head_rows(h), :], ones_rows], axis=0)
        pv = jnp.dot(v_ext, jnp.exp2(z - m).astype(BF16), preferred_element_type=F32)
        acc = pv if acc is None else acc + pv
        state[h, qi][1] = acc
        if kj == 0:
            bad[0] = jnp.maximum(bad[0], jnp.where(jnp.isfinite(acc), 0, 1))
            out = acc[:HEAD_DIM] / acc[HEAD_DIM:HEAD_DIM + 1]
            o_ref[0, qi, head_rows(h), :] = _head_rmsnorm_t(out, g_ref[h]).astype(BF16)
            del state[h, qi]

    def step(s):
        if s < len(tiles):
            scores(s)
        if s >= DIL_SCORE_LEAD:
            attend(s - DIL_SCORE_LEAD)

    def overflowed():
        return jnp.max(bad[0])

    return [functools.partial(step, s) for s in range(len(tiles) + DIL_SCORE_LEAD)], overflowed


def _dil_exact_program(q_ref, k_ref, v_ref, bias_ref, g_ref, o_ref, z_ref):
    nt, t = q_ref.shape[1], q_ref.shape[3]
    n_heads = q_ref.shape[2] // HEAD_DIM
    units = [(h, qi) for h in range(n_heads) for qi in range(nt)]
    ones_rows = jnp.ones((BF16_SUBLANES, t), BF16)
    m_of, acc_of = {}, {}

    def head_rows(h):
        return slice(h * HEAD_DIM, (h + 1) * HEAD_DIM)

    def score_tile(u, kj):
        h, qi = units[u]
        z = lax.dot_general(k_ref[0, kj, head_rows(h), :], q_ref[0, qi, head_rows(h), :], _CONTRACT_ROWS,
                            preferred_element_type=F32)
        z = z + bias_ref[h, qi - kj]
        z_ref[u % 2, kj] = z
        zm = jnp.max(z, axis=0, keepdims=True)
        m_of[u] = zm if kj == 0 else jnp.maximum(m_of[u], zm)

    def attend_tile(u, kj):
        h, qi = units[u]
        p = jnp.exp2(z_ref[u % 2, kj] - m_of[u]).astype(BF16)
        v_ext = jnp.concatenate([v_ref[0, kj, head_rows(h), :], ones_rows], axis=0)
        pv = jnp.dot(v_ext, p, preferred_element_type=F32)
        acc = pv if kj == 0 else acc_of[u] + pv
        acc_of[u] = acc
        if kj == qi:
            out = acc[:HEAD_DIM] / acc[HEAD_DIM:HEAD_DIM + 1]
            o_ref[0, qi, head_rows(h), :] = _head_rmsnorm_t(out, g_ref[h]).astype(BF16)
            del acc_of[u], m_of[u]

    steps = [functools.partial(score_tile, 0, kj) for kj in range(units[0][1] + 1)]
    for u, (_, qi) in enumerate(units):
        if u + 1 < len(units):
            steps += [functools.partial(score_tile, u + 1, kj) for kj in range(units[u + 1][1] + 1)]
        steps += [functools.partial(attend_tile, u, kj) for kj in range(qi + 1)]
    return steps


def _attention_kernel(qa_ref, ka_ref, va_ref, qb_ref, kb_ref, vb_ref, u_ref, bias_ref, ga_ref, gb_ref,
                      oa_ref, ob_ref, z_ref):
    sb_steps = _sb_program(qa_ref, ka_ref, va_ref, u_ref, ga_ref, oa_ref)
    dil_steps, dil_overflowed = _dil_fast_program(qb_ref, kb_ref, vb_ref, bias_ref, gb_ref, ob_ref)
    n = max(len(sb_steps), len(dil_steps))
    done_sb = done_dil = 0
    for i in range(1, n + 1):
        for step in sb_steps[done_sb:len(sb_steps) * i // n]:
            step()
        for step in dil_steps[done_dil:len(dil_steps) * i // n]:
            step()
        done_sb, done_dil = len(sb_steps) * i // n, len(dil_steps) * i // n

    @pl.when(dil_overflowed() > 0)
    def _():
        for step in _dil_exact_program(qb_ref, kb_ref, vb_ref, bias_ref, gb_ref, ob_ref, z_ref):
            step()


def _attention(qkv_t, bias_tiles, g_sb, g_dil, sb_blk, dil_blk):
    b, nt, _, t = qkv_t.shape
    nh = g_sb.shape[0]
    assert g_dil.shape[0] == nh
    hps = HEADS_PER_STEP
    blk = (1, nt, hps * HEAD_DIM, t)

    def rows(first_block):
        return pl.BlockSpec(blk, lambda h, i: (i, 0, first_block // hps + h, 0))

    heads = pl.BlockSpec((hps, HEAD_DIM, 1), lambda h, i: (h, 0, 0))
    out_shape = jax.ShapeDtypeStruct((b, nt, nh * HEAD_DIM, t), BF16)
    return pl.pallas_call(
        _attention_kernel,
        grid=(nh // hps, b),
        in_specs=[rows(r) for r in sb_blk] + [rows(r) for r in dil_blk] + [
            _resident((SB_SUB_TILE, 2 * SB_SUB_TILE), lambda h, i: (0, 0)),
            pl.BlockSpec((hps, nt, t, t), lambda h, i: (h, 0, 0, 0)),
            heads, heads,
        ],
        out_specs=(pl.BlockSpec(blk, lambda h, i: (i, 0, h, 0)),) * 2,
        out_shape=(out_shape, out_shape),
        scratch_shapes=[pltpu.VMEM((2, nt, t, t), F32)],
        compiler_params=_cparams(2),
        name="attention",
    )(*([qkv_t] * 6), _suffix_matrix(SB_SUB_TILE), bias_tiles,
      g_sb.reshape(nh, HEAD_DIM, 1), g_dil.reshape(nh, HEAD_DIM, 1))


def kernel(x, c, w_ada, b_ada, g_ffn1, w1_gate, w1_up, w1_down, g_mix, w_in, g_sb_out, g_dil_out, w_out, rel_bias, g_ffn2, w2_gate, w2_up, w2_down, g_final):
    depth = w_ada.shape[0]
    seq = x.shape[1]
    nh_sb = g_sb_out.shape[1]
    nh_dil = g_dil_out.shape[1]
    d_sb = nh_sb * HEAD_DIM
    sb_blk = (0, nh_sb, 2 * nh_sb)
    dil_blk = (3 * nh_sb, 3 * nh_sb + nh_dil, 3 * nh_sb + 2 * nh_dil)
    d_dil = nh_dil * HEAD_DIM
    assert d_sb % PROJ_ROW_CHUNK == 0 and d_dil % PROJ_ROW_CHUNK == 0
    assert g_sb_out.shape[2] == HEAD_DIM and g_dil_out.shape[2] == HEAD_DIM and w_in.shape[2] == 3 * (d_sb + d_dil)
    assert seq % ATT_TILE == 0 and seq <= MAX_DISTANCE and nh_sb % HEADS_PER_STEP == 0
    q_chunks = tuple(range(d_sb // PROJ_ROW_CHUNK)) + tuple(
        range(3 * d_sb // PROJ_ROW_CHUNK, (3 * d_sb + d_dil) // PROJ_ROW_CHUNK))
    bias_tiles = _dilated_bias_tiles(rel_bias, seq, ATT_TILE)
    for l in range(depth):
        mod = _adaln_mod(c, w_ada[l], b_ada[l])
        x, qkv_t = _ffn(x, mod, g_ffn1[l], w1_gate[l].astype(BF16), w1_up[l].astype(BF16), w1_down[l].astype(BF16), 0,
                        proj=(g_mix[l], w_in[l].T.astype(BF16), 3, q_chunks))
        o_sb, o_dil = _attention(qkv_t, bias_tiles, g_sb_out[l], g_dil_out[l], sb_blk, dil_blk)
        w_o = w_out[l].astype(BF16)
        last = l == depth - 1
        x = _ffn(x, mod, g_ffn2[l], w2_gate[l].astype(BF16), w2_up[l].astype(BF16), w2_down[l].astype(BF16), 6,
                 mixer=(o_sb, o_dil, w_o[:d_sb], w_o[d_sb:], 3), g_final=g_final if last else None)
    return x
```

```python
import functools
import math

import jax
import jax.numpy as jnp
import numpy as np
from jax import lax
from jax.experimental import pallas as pl
from jax.experimental.pallas import tpu as pltpu

F32 = jnp.float32
BF16 = jnp.bfloat16

HEAD_DIM = 64
DIL_CONFIGS = ((128, 1), (512, 4), (2048, 16))
MAX_DISTANCE = 2048
N_MOD = 9
EPS = 1e-6
NEG_INF = -1e30

ATT_TILE = 256
SB_SUB_TILE = 128
HEADS_PER_STEP = 2
SOFTPLUS_LINEAR_FROM = 64.0
SB_STAGE_LEADS = (1, 1)
DIL_SCORE_LEAD = 5
LOG2E = 1.4426950408889634
BF16_SUBLANES = 16
FFN_TOKEN_TILE = 512
FFN_OUT_TOKEN_TILE = 1024
PROJ_ROW_CHUNK = 512
MOD_COL_TILE = 2304
VMEM_LIMIT = 56 * 1024 * 1024


def _cparams(n_grid):
    return pltpu.CompilerParams(dimension_semantics=("arbitrary",) * n_grid, vmem_limit_bytes=VMEM_LIMIT)


def _resident(block_shape, index_map):
    return pl.BlockSpec(block_shape, index_map, pipeline_mode=pl.Buffered(1))


def _mod_kernel(c_ref, w_ref, b_ref, o_ref):
    c = c_ref[...]
    s = c * jax.nn.sigmoid(c)
    o_ref[...] = jnp.dot(s.astype(BF16), w_ref[...].astype(BF16), preferred_element_type=F32) + b_ref[...]


def _adaln_mod(c, w_ada, b_ada):
    b, d = c.shape
    n = w_ada.shape[1]
    tn = MOD_COL_TILE
    assert n % tn == 0 and n == N_MOD * d
    out = pl.pallas_call(
        _mod_kernel,
        grid=(n // tn,),
        in_specs=[
            pl.BlockSpec((b, d), lambda j: (0, 0)),
            pl.BlockSpec((d, tn), lambda j: (0, j)),
            pl.BlockSpec((1, tn), lambda j: (0, j)),
        ],
        out_specs=pl.BlockSpec((b, tn), lambda j: (0, j)),
        out_shape=jax.ShapeDtypeStruct((b, n), F32),
        compiler_params=_cparams(1),
        name="adaln_mod",
    )(c, w_ada, b_ada.reshape(1, n))
    return out.reshape(b, N_MOD, d)


def _modulate(x, g, shift, scale):
    y = x * lax.rsqrt(jnp.mean(x * x, axis=-1, keepdims=True) + EPS)
    return (y * g) * (1.0 + scale) + shift


def _ffn_kernel(*refs, mod_base, mixer_base, proj_base, q_chunks, final):
    x_ref, mod_ref, g_ref, wg_ref, wu_ref, wd_ref = refs[:6]
    rest = list(refs[6:])
    qkv_ref = rest.pop() if proj_base is not None else None
    o_ref = rest.pop()
    gf_ref = rest.pop() if final else None
    gmix_ref, win_ref = (rest.pop(-2), rest.pop()) if proj_base is not None else (None, None)
    mixer_refs = rest if mixer_base is not None else None
    shift = mod_ref[0, mod_base:mod_base + 1, :]
    scale = mod_ref[0, mod_base + 1:mod_base + 2, :]
    gate = mod_ref[0, mod_base + 2:mod_base + 3, :]
    sub = ATT_TILE
    n_sub = x_ref.shape[1] // sub
    xs, hs, acts = {}, {}, {}

    def prologue(i):
        x = x_ref[0, i * sub:(i + 1) * sub, :]
        if mixer_refs is not None:
            osb_ref, odil_ref, wsb_ref, wdil_ref = mixer_refs
            y = lax.dot_general(osb_ref[0, i], wsb_ref[...], _CONTRACT_ROWS, preferred_element_type=F32)
            y = y + lax.dot_general(odil_ref[0, i], wdil_ref[...], _CONTRACT_ROWS, preferred_element_type=F32)
            x = x + mod_ref[0, mixer_base + 2:mixer_base + 3, :] * y
        xs[i] = x
        hs[i] = _modulate(x, g_ref[...], shift, scale).astype(BF16)

    def expand(i):
        hb = hs.pop(i)
        gg = jnp.dot(hb, wg_ref[...], preferred_element_type=F32)
        uu = jnp.dot(hb, wu_ref[...], preferred_element_type=F32)
        acts[i] = ((gg * jax.nn.sigmoid(gg)) * uu).astype(BF16)

    def contract(i):
        out = xs.pop(i) + (0.5 * gate) * jnp.dot(acts.pop(i), wd_ref[...], preferred_element_type=F32)
        if proj_base is not None:
            xs[i] = out
        if final:
            out = out * lax.rsqrt(jnp.mean(out * out, axis=-1, keepdims=True) + EPS) * gf_ref[...]
        o_ref[0, i * sub:(i + 1) * sub, :] = out

    def project(i):
        hb = _modulate(xs.pop(i), gmix_ref[...], mod_ref[0, proj_base:proj_base + 1, :],
                       mod_ref[0, proj_base + 1:proj_base + 2, :]).astype(BF16)
        q_scale = HEAD_DIM ** -0.5 * LOG2E
        for r in range(win_ref.shape[0] // PROJ_ROW_CHUNK):
            rows = slice(r * PROJ_ROW_CHUNK, (r + 1) * PROJ_ROW_CHUNK)
            res = lax.dot_general(win_ref[rows, :], hb, (((1,), (1,)), ((), ())), preferred_element_type=F32)
            if r in q_chunks:
                res = res * q_scale
            qkv_ref[0, i, rows, :] = res.astype(BF16)

    for i in range(n_sub):
        prologue(i)
    for i in range(n_sub):
        expand(i)
    for i in range(n_sub):
        contract(i)
    if proj_base is not None:
        for i in range(n_sub):
            project(i)


def _ffn(x, mod, g, wg, wu, wd, mod_base, mixer=None, proj=None, g_final=None):
    b, s, d = x.shape
    dff = wg.shape[1]
    tm = FFN_TOKEN_TILE if proj is not None else FFN_OUT_TOKEN_TILE
    assert s % tm == 0 and tm % ATT_TILE == 0
    final = g_final is not None
    in_specs = [
        pl.BlockSpec((1, tm, d), lambda i, j: (i, j, 0)),
        pl.BlockSpec((1, N_MOD, d), lambda i, j: (i, 0, 0)),
        _resident((1, d), lambda i, j: (0, 0)),
        _resident((d, dff), lambda i, j: (0, 0)),
        _resident((d, dff), lambda i, j: (0, 0)),
        _resident((dff, d), lambda i, j: (0, 0)),
    ]
    args = [x, mod, g.reshape(1, d), wg, wu, wd]
    mixer_base = None
    if mixer is not None:
        o_sb, o_dil, w_sb, w_dil, mixer_base = mixer
        t = o_sb.shape[3]
        for o_heads in (o_sb, o_dil):
            in_specs.append(pl.BlockSpec((1, tm // t, o_heads.shape[2], t), lambda i, j: (i, j, 0, 0)))
            args.append(o_heads)
        for w_heads in (w_sb, w_dil):
            in_specs.append(_resident(w_heads.shape, lambda i, j: (0, 0)))
            args.append(w_heads)
    out_specs = pl.BlockSpec((1, tm, d), lambda i, j: (i, j, 0))
    out_shape = jax.ShapeDtypeStruct((b, s, d), F32)
    proj_base, q_chunks = None, ()
    if proj is not None:
        g_mix, w_in_t, proj_base, q_chunks = proj
        n = w_in_t.shape[0]
        in_specs += [_resident((1, d), lambda i, j: (0, 0)), _resident((n, d), lambda i, j: (0, 0))]
        args += [g_mix.reshape(1, d), w_in_t]
        out_specs = (out_specs, pl.BlockSpec((1, tm // ATT_TILE, n, ATT_TILE), lambda i, j: (i, j, 0, 0)))
        out_shape = (out_shape, jax.ShapeDtypeStruct((b, s // ATT_TILE, n, ATT_TILE), BF16))
    if final:
        in_specs.append(_resident((1, d), lambda i, j: (0, 0)))
        args.append(g_final.reshape(1, d))
    return pl.pallas_call(
        functools.partial(_ffn_kernel, mod_base=mod_base, mixer_base=mixer_base, proj_base=proj_base,
                          q_chunks=q_chunks, final=final),
        grid=(b, s // tm),
        in_specs=in_specs,
        out_specs=out_specs,
        out_shape=out_shape,
        compiler_params=_cparams(2),
        name="ffn_mixer_out" if mixer is not None else "ffn_qkv" if proj is not None else "ffn",
    )(*args)


def _head_rmsnorm_t(o, g_col):
    return (o * lax.rsqrt(jnp.mean(o * o, axis=0, keepdims=True) + EPS)) * g_col


_CONTRACT_ROWS = (((0,), (0,)), ((), ()))


def _sb_program(q_ref, k_ref, v_ref, u_ref, g_ref, o_ref):
    nt, t = q_ref.shape[1], q_ref.shape[3]
    n_heads = q_ref.shape[2] // HEAD_DIM
    ks = u_ref.shape[0]
    sub = t // ks
    ucat = u_ref[...]
    tiles = [(h, qi, kj) for h in range(n_heads) for qi in range(nt) for kj in range(qi, -1, -1)]
    subs = range(sub - 1, -1, -1)
    z_of, cat_of, incl_of, w_of = {}, {}, {}, {}
    state = {}

    def head_rows(h):
        return slice(h * HEAD_DIM, (h + 1) * HEAD_DIM)

    def first_lane(qi, kj, si):
        return si * ks if kj == qi else 0

    def causal(lanes):
        return lax.broadcasted_iota(jnp.int32, (ks, lanes), 0) < lax.broadcasted_iota(jnp.int32, (ks, lanes), 1)

    def scores(n):
        h, qi, kj = tiles[n]
        q_t = q_ref[0, qi, head_rows(h), :]
        z_of[n] = [lax.dot_general(k_ref[0, kj, head_rows(h), si * ks:(si + 1) * ks],
                                   q_t[:, first_lane(qi, kj, si):], _CONTRACT_ROWS,
                                   preferred_element_type=F32) for si in range(sub)]

    def softplus(n):
        h, qi, kj = tiles[n]
        cats = []
        for z in z_of[n]:
            sp = jnp.maximum(z, jnp.log(1.0 + jnp.exp2(jnp.minimum(z, SOFTPLUS_LINEAR_FROM))) * LOG2E)
            if kj == qi:
                sp = jnp.where(causal(z.shape[1]), sp, 0.0)
            cats.append(sp.astype(BF16))
        cat_of[n] = cats

    def cumsum(n):
        incl_of[n] = [jnp.dot(ucat, c, preferred_element_type=F32) for c in cat_of.pop(n)]

    def weights(n):
        h, qi, kj = tiles[n]
        carry = state.setdefault((h, qi), [None, None])[0]
        zs, incls = z_of.pop(n), incl_of.pop(n)
        parts = [None] * sub
        for si in subs:
            skipped = first_lane(qi, kj, si)
            arg = zs[si] - incls[si]
            if carry is not None:
                arg = arg - carry[:, skipped:]
            w = jnp.exp2(arg)
            total = incls[si][0:1, :]
            if kj == qi:
                w = jnp.where(causal(w.shape[1]), w, 0.0)
            if skipped:
                w = jnp.concatenate([jnp.zeros((ks, skipped), F32), w], axis=1)
                total = jnp.concatenate([jnp.zeros((1, skipped), F32), total], axis=1)
            parts[si] = w.astype(BF16)
            carry = total if carry is None else carry + total
        state[h, qi][0] = carry
        w_of[n] = jnp.concatenate(parts, axis=0)

    def values(n):
        h, qi, kj = tiles[n]
        pv = jnp.dot(v_ref[0, kj, head_rows(h), :], w_of.pop(n), preferred_element_type=F32)
        acc = state[h, qi][1]
        acc = pv if acc is None else acc + pv
        state[h, qi][1] = acc
        if kj == 0:
            o_ref[0, qi, head_rows(h), :] = _head_rmsnorm_t(acc, g_ref[h]).astype(BF16)
            del state[h, qi]

    n_tiles = len(tiles)

    lead_a, lead_b = SB_STAGE_LEADS

    def step(s):
        if s < n_tiles:
            scores(s)
        if 0 <= s - lead_a < n_tiles:
            softplus(s - lead_a)
            cumsum(s - lead_a)
        if 0 <= s - lead_a - lead_b < n_tiles:
            weights(s - lead_a - lead_b)
            values(s - lead_a - lead_b)

    return [functools.partial(step, s) for s in range(n_tiles + lead_a + lead_b)]


def _suffix_matrix(n):
    return jnp.asarray(np.arange(n)[None, :] >= np.arange(n)[:, None], dtype=BF16)


def _t5_causal_bucket(n, n_buckets):
    max_exact = n_buckets // 2
    nf = np.maximum(n, 1).astype(np.float32)
    large = max_exact + (np.log(nf / max_exact) / math.log(MAX_DISTANCE / max_exact)
                         * (n_buckets - max_exact)).astype(np.int32)
    large = np.minimum(large, n_buckets - 1)
    return np.where(n < max_exact, n, large).astype(np.int32)


def _dilated_bias_tiles(rel_bias, seq, t):
    dist = np.arange(seq)
    mult = np.zeros(seq, np.float32)
    for window, dilation in DIL_CONFIGS:
        mult += ((dist % dilation == 0) & (dist <= window)).astype(np.float32)
    bucket = _t5_causal_bucket(dist, rel_bias.shape[0])
    per_dist = (rel_bias[bucket].astype(F32) + jnp.log(jnp.maximum(jnp.asarray(mult), 1.0))[:, None]) * LOG2E
    per_dist = jnp.where(jnp.asarray(mult > 0)[:, None], per_dist, NEG_INF)
    nh = rel_bias.shape[1]
    table = jnp.concatenate([jnp.full((nh, t), NEG_INF, F32), per_dist.T], axis=1).reshape(nh, 1, seq + t)
    return pl.pallas_call(
        _bias_tile_kernel,
        grid=(nh,),
        in_specs=[pl.BlockSpec((1, 1, seq + t), lambda h: (h, 0, 0))],
        out_specs=pl.BlockSpec((1, seq // t, t, t), lambda h: (h, 0, 0, 0)),
        out_shape=jax.ShapeDtypeStruct((nh, seq // t, t, t), F32),
        compiler_params=_cparams(1),
        name="dilated_bias_tiles",
    )(table)


def _bias_tile_kernel(tab_ref, o_ref):
    no, t = o_ref.shape[1], o_ref.shape[2]
    for o in range(no):
        window = jnp.broadcast_to(tab_ref[0, :, o * t:(o + 2) * t], (t, 2 * t))
        skewed = pltpu.roll(window, 0, 1, stride=1, stride_axis=0)
        o_ref[0, o] = skewed[:, t:]


def _dil_fast_program(q_ref, k_ref, v_ref, bias_ref, g_ref, o_ref):
    nt, t = q_ref.shape[1], q_ref.shape[3]
    n_heads = q_ref.shape[2] // HEAD_DIM
    ones_rows = jnp.ones((BF16_SUBLANES, t), BF16)
    tiles = [(h, qi, kj) for h in range(n_heads) for qi in range(nt) for kj in range(qi, -1, -1)]
    z_of, state = {}, {}
    bad = [jnp.zeros((HEAD_DIM + BF16_SUBLANES, t), jnp.int32)]

    def head_rows(h):
        return slice(h * HEAD_DIM, (h + 1) * HEAD_DIM)

    def scores(n):
        h, qi, kj = tiles[n]
        z = lax.dot_general(k_ref[0, kj, head_rows(h), :], q_ref[0, qi, head_rows(h), :], _CONTRACT_ROWS,
                            preferred_element_type=F32)
        z_of[n] = z + bias_ref[h, qi - kj]

    def attend(n):
        h, qi, kj = tiles[n]
        z = z_of.pop(n)
        if kj == qi:
            state[h, qi] = [jnp.max(z, axis=0, keepdims=True), None]
        m, acc = state[h, qi]
        v_ext = jnp.concatenate([v_ref[0, kj, head_rows(h), :], ones_rows], axis=0)
        pv = jnp.dot(v_ext, jnp.exp2(z - m).astype(BF16), preferred_element_type=F32)
        acc = pv if acc is None else acc + pv
        state[h, qi][1] = acc
        if kj == 0:
            bad[0] = jnp.maximum(bad[0], jnp.where(jnp.isfinite(acc), 0, 1))
            out = acc[:HEAD_DIM] / acc[HEAD_DIM:HEAD_DIM + 1]
            o_ref[0, qi, head_rows(h), :] = _head_rmsnorm_t(out, g_ref[h]).astype(BF16)
            del state[h, qi]

    def step(s):
        if s < len(tiles):
            scores(s)
        if s >= DIL_SCORE_LEAD:
            attend(s - DIL_SCORE_LEAD)

    def overflowed():
        return jnp.max(bad[0])

    return [functools.partial(step, s) for s in range(len(tiles) + DIL_SCORE_LEAD)], overflowed


def _dil_exact_program(q_ref, k_ref, v_ref, bias_ref, g_ref, o_ref, z_ref):
    nt, t = q_ref.shape[1], q_ref.shape[3]
    n_heads = q_ref.shape[2] // HEAD_DIM
    units = [(h, qi) for h in range(n_heads) for qi in range(nt)]
    ones_rows = jnp.ones((BF16_SUBLANES, t), BF16)
    m_of, acc_of = {}, {}

    def head_rows(h):
        return slice(h * HEAD_DIM, (h + 1) * HEAD_DIM)

    def score_tile(u, kj):
        h, qi = units[u]
        z = lax.dot_general(k_ref[0, kj, head_rows(h), :], q_ref[0, qi, head_rows(h), :], _CONTRACT_ROWS,
                            preferred_element_type=F32)
        z = z + bias_ref[h, qi - kj]
        z_ref[u % 2, kj] = z
        zm = jnp.max(z, axis=0, keepdims=True)
        m_of[u] = zm if kj == 0 else jnp.maximum(m_of[u], zm)

    def attend_tile(u, kj):
        h, qi = units[u]
        p = jnp.exp2(z_ref[u % 2, kj] - m_of[u]).astype(BF16)
        v_ext = jnp.concatenate([v_ref[0, kj, head_rows(h), :], ones_rows], axis=0)
        pv = jnp.dot(v_ext, p, preferred_element_type=F32)
        acc = pv if kj == 0 else acc_of[u] + pv
        acc_of[u] = acc
        if kj == qi:
            out = acc[:HEAD_DIM] / acc[HEAD_DIM:HEAD_DIM + 1]
            o_ref[0, qi, head_rows(h), :] = _head_rmsnorm_t(out, g_ref[h]).astype(BF16)
            del acc_of[u], m_of[u]

    steps = [functools.partial(score_tile, 0, kj) for kj in range(units[0][1] + 1)]
    for u, (_, qi) in enumerate(units):
        if u + 1 < len(units):
            steps += [functools.partial(score_tile, u + 1, kj) for kj in range(units[u + 1][1] + 1)]
        steps += [functools.partial(attend_tile, u, kj) for kj in range(qi + 1)]
    return steps


def _attention_kernel(qa_ref, ka_ref, va_ref, qb_ref, kb_ref, vb_ref, u_ref, bias_ref, ga_ref, gb_ref,
                      oa_ref, ob_ref, z_ref):
    sb_steps = _sb_program(qa_ref, ka_ref, va_ref, u_ref, ga_ref, oa_ref)
    dil_steps, dil_overflowed = _dil_fast_program(qb_ref, kb_ref, vb_ref, bias_ref, gb_ref, ob_ref)
    n = max(len(sb_steps), len(dil_steps))
    done_sb = done_dil = 0
    for i in range(1, n + 1):
        for step in sb_steps[done_sb:len(sb_steps) * i // n]:
            step()
        for step in dil_steps[done_dil:len(dil_steps) * i // n]:
            step()
        done_sb, done_dil = len(sb_steps) * i // n, len(dil_steps) * i // n

    @pl.when(dil_overflowed() > 0)
    def _():
        for step in _dil_exact_program(qb_ref, kb_ref, vb_ref, bias_ref, gb_ref, ob_ref, z_ref):
            step()


def _attention(qkv_t, bias_tiles, g_sb, g_dil, sb_blk, dil_blk):
    b, nt, _, t = qkv_t.shape
    nh = g_sb.shape[0]
    assert g_dil.shape[0] == nh
    hps = HEADS_PER_STEP
    blk = (1, nt, hps * HEAD_DIM, t)

    def rows(first_block):
        return pl.BlockSpec(blk, lambda h, i: (i, 0, first_block // hps + h, 0))

    heads = pl.BlockSpec((hps, HEAD_DIM, 1), lambda h, i: (h, 0, 0))
    out_shape = jax.ShapeDtypeStruct((b, nt, nh * HEAD_DIM, t), BF16)
    return pl.pallas_call(
        _attention_kernel,
        grid=(nh // hps, b),
        in_specs=[rows(r) for r in sb_blk] + [rows(r) for r in dil_blk] + [
            _resident((SB_SUB_TILE, SB_SUB_TILE), lambda h, i: (0, 0)),
            pl.BlockSpec((hps, nt, t, t), lambda h, i: (h, 0, 0, 0)),
            heads, heads,
        ],
        out_specs=(pl.BlockSpec(blk, lambda h, i: (i, 0, h, 0)),) * 2,
        out_shape=(out_shape, out_shape),
        scratch_shapes=[pltpu.VMEM((2, nt, t, t), F32)],
        compiler_params=_cparams(2),
        name="attention",
    )(*([qkv_t] * 6), _suffix_matrix(SB_SUB_TILE), bias_tiles,
      g_sb.reshape(nh, HEAD_DIM, 1), g_dil.reshape(nh, HEAD_DIM, 1))


def kernel(x, c, w_ada, b_ada, g_ffn1, w1_gate, w1_up, w1_down, g_mix, w_in, g_sb_out, g_dil_out, w_out, rel_bias, g_ffn2, w2_gate, w2_up, w2_down, g_final):
    depth = w_ada.shape[0]
    seq = x.shape[1]
    nh_sb = g_sb_out.shape[1]
    nh_dil = g_dil_out.shape[1]
    d_sb = nh_sb * HEAD_DIM
    sb_blk = (0, nh_sb, 2 * nh_sb)
    dil_blk = (3 * nh_sb, 3 * nh_sb + nh_dil, 3 * nh_sb + 2 * nh_dil)
    d_dil = nh_dil * HEAD_DIM
    assert d_sb % PROJ_ROW_CHUNK == 0 and d_dil % PROJ_ROW_CHUNK == 0
    assert g_sb_out.shape[2] == HEAD_DIM and g_dil_out.shape[2] == HEAD_DIM and w_in.shape[2] == 3 * (d_sb + d_dil)
    assert seq % ATT_TILE == 0 and seq <= MAX_DISTANCE and nh_sb % HEADS_PER_STEP == 0
    q_chunks = tuple(range(d_sb // PROJ_ROW_CHUNK)) + tuple(
        range(3 * d_sb // PROJ_ROW_CHUNK, (3 * d_sb + d_dil) // PROJ_ROW_CHUNK))
    bias_tiles = _dilated_bias_tiles(rel_bias, seq, ATT_TILE)
    for l in range(depth):
        mod = _adaln_mod(c, w_ada[l], b_ada[l])
        x, qkv_t = _ffn(x, mod, g_ffn1[l], w1_gate[l].astype(BF16), w1_up[l].astype(BF16), w1_down[l].astype(BF16), 0,
                        proj=(g_mix[l], w_in[l].T.astype(BF16), 3, q_chunks))
        o_sb, o_dil = _attention(qkv_t, bias_tiles, g_sb_out[l], g_dil_out[l], sb_blk, dil_blk)
        w_o = w_out[l].astype(BF16)
        last = l == depth - 1
        x = _ffn(x, mod, g_ffn2[l], w2_gate[l].astype(BF16), w2_up[l].astype(BF16), w2_down[l].astype(BF16), 6,
                 mixer=(o_sb, o_dil, w_o[:d_sb], w_o[d_sb:], 3), g_final=g_final if last else None)
    return x
```

```python
import functools
import math

import jax
import jax.numpy as jnp
import numpy as np
from jax import lax
from jax.experimental import pallas as pl
from jax.experimental.pallas import tpu as pltpu

F32 = jnp.float32
BF16 = jnp.bfloat16

HEAD_DIM = 64
DIL_CONFIGS = ((128, 1), (512, 4), (2048, 16))
MAX_DISTANCE = 2048
N_MOD = 9
EPS = 1e-6
NEG_INF = -1e30

ATT_TILE = 256
SB_SUB_TILE = 128
HEADS_PER_STEP = 2
SOFTPLUS_LINEAR_FROM = 64.0
SB_STAGE_LEADS = (1, 1)
DIL_SCORE_LEAD = 4
LOG2E = 1.4426950408889634
BF16_SUBLANES = 16
FFN_TOKEN_TILE = 512
FFN_OUT_TOKEN_TILE = 1024
PROJ_ROW_CHUNK = 512
MOD_COL_TILE = 2304
VMEM_LIMIT = 56 * 1024 * 1024


def _cparams(n_grid):
    return pltpu.CompilerParams(dimension_semantics=("arbitrary",) * n_grid, vmem_limit_bytes=VMEM_LIMIT)


def _resident(block_shape, index_map):
    return pl.BlockSpec(block_shape, index_map, pipeline_mode=pl.Buffered(1))


def _mod_kernel(c_ref, w_ref, b_ref, o_ref):
    c = c_ref[...]
    s = c * jax.nn.sigmoid(c)
    o_ref[...] = jnp.dot(s.astype(BF16), w_ref[...].astype(BF16), preferred_element_type=F32) + b_ref[...]


def _adaln_mod(c, w_ada, b_ada):
    b, d = c.shape
    n = w_ada.shape[1]
    tn = MOD_COL_TILE
    assert n % tn == 0 and n == N_MOD * d
    out = pl.pallas_call(
        _mod_kernel,
        grid=(n // tn,),
        in_specs=[
            pl.BlockSpec((b, d), lambda j: (0, 0)),
            pl.BlockSpec((d, tn), lambda j: (0, j)),
            pl.BlockSpec((1, tn), lambda j: (0, j)),
        ],
        out_specs=pl.BlockSpec((b, tn), lambda j: (0, j)),
        out_shape=jax.ShapeDtypeStruct((b, n), F32),
        compiler_params=_cparams(1),
        name="adaln_mod",
    )(c, w_ada, b_ada.reshape(1, n))
    return out.reshape(b, N_MOD, d)


def _modulate(x, g, shift, scale):
    y = x * lax.rsqrt(jnp.mean(x * x, axis=-1, keepdims=True) + EPS)
    return (y * g) * (1.0 + scale) + shift


def _ffn_kernel(*refs, mod_base, mixer_base, proj_base, q_chunks, final):
    x_ref, mod_ref, g_ref, wg_ref, wu_ref, wd_ref = refs[:6]
    rest = list(refs[6:])
    qkv_ref = rest.pop() if proj_base is not None else None
    o_ref = rest.pop()
    gf_ref = rest.pop() if final else None
    gmix_ref, win_ref = (rest.pop(-2), rest.pop()) if proj_base is not None else (None, None)
    mixer_refs = rest if mixer_base is not None else None
    shift = mod_ref[0, mod_base:mod_base + 1, :]
    scale = mod_ref[0, mod_base + 1:mod_base + 2, :]
    gate = mod_ref[0, mod_base + 2:mod_base + 3, :]
    sub = ATT_TILE
    n_sub = x_ref.shape[1] // sub
    xs, hs, acts = {}, {}, {}

    def prologue(i):
        x = x_ref[0, i * sub:(i + 1) * sub, :]
        if mixer_refs is not None:
            osb_ref, odil_ref, wsb_ref, wdil_ref = mixer_refs
            y = lax.dot_general(osb_ref[0, i], wsb_ref[...], _CONTRACT_ROWS, preferred_element_type=F32)
            y = y + lax.dot_general(odil_ref[0, i], wdil_ref[...], _CONTRACT_ROWS, preferred_element_type=F32)
            x = x + mod_ref[0, mixer_base + 2:mixer_base + 3, :] * y
        xs[i] = x
        hs[i] = _modulate(x, g_ref[...], shift, scale).astype(BF16)

    def expand(i):
        hb = hs.pop(i)
        gg = jnp.dot(hb, wg_ref[...], preferred_element_type=F32)
        uu = jnp.dot(hb, wu_ref[...], preferred_element_type=F32)
        acts[i] = ((gg * jax.nn.sigmoid(gg)) * uu).astype(BF16)

    def contract(i):
        out = xs.pop(i) + (0.5 * gate) * jnp.dot(acts.pop(i), wd_ref[...], preferred_element_type=F32)
        if proj_base is not None:
            xs[i] = out
        if final:
            out = out * lax.rsqrt(jnp.mean(out * out, axis=-1, keepdims=True) + EPS) * gf_ref[...]
        o_ref[0, i * sub:(i + 1) * sub, :] = out

    def project(i):
        hb = _modulate(xs.pop(i), gmix_ref[...], mod_ref[0, proj_base:proj_base + 1, :],
                       mod_ref[0, proj_base + 1:proj_base + 2, :]).astype(BF16)
        q_scale = HEAD_DIM ** -0.5 * LOG2E
        for r in range(win_ref.shape[0] // PROJ_ROW_CHUNK):
            rows = slice(r * PROJ_ROW_CHUNK, (r + 1) * PROJ_ROW_CHUNK)
            res = lax.dot_general(win_ref[rows, :], hb, (((1,), (1,)), ((), ())), preferred_element_type=F32)
            if r in q_chunks:
                res = res * q_scale
            qkv_ref[0, i, rows, :] = res.astype(BF16)

    for i in range(n_sub):
        prologue(i)
    for i in range(n_sub):
        expand(i)
    for i in range(n_sub):
        contract(i)
    if proj_base is not None:
        for i in range(n_sub):
            project(i)


def _ffn(x, mod, g, wg, wu, wd, mod_base, mixer=None, proj=None, g_final=None):
    b, s, d = x.shape
    dff = wg.shape[1]
    tm = FFN_TOKEN_TILE if proj is not None else FFN_OUT_TOKEN_TILE
    assert s % tm == 0 and tm % ATT_TILE == 0
    final = g_final is not None
    in_specs = [
        pl.BlockSpec((1, tm, d), lambda i, j: (i, j, 0)),
        pl.BlockSpec((1, N_MOD, d), lambda i, j: (i, 0, 0)),
        _resident((1, d), lambda i, j: (0, 0)),
        _resident((d, dff), lambda i, j: (0, 0)),
        _resident((d, dff), lambda i, j: (0, 0)),
        _resident((dff, d), lambda i, j: (0, 0)),
    ]
    args = [x, mod, g.reshape(1, d), wg, wu, wd]
    mixer_base = None
    if mixer is not None:
        o_sb, o_dil, w_sb, w_dil, mixer_base = mixer
        t = o_sb.shape[3]
        for o_heads in (o_sb, o_dil):
            in_specs.append(pl.BlockSpec((1, tm // t, o_heads.shape[2], t), lambda i, j: (i, j, 0, 0)))
            args.append(o_heads)
        for w_heads in (w_sb, w_dil):
            in_specs.append(_resident(w_heads.shape, lambda i, j: (0, 0)))
            args.append(w_heads)
    out_specs = pl.BlockSpec((1, tm, d), lambda i, j: (i, j, 0))
    out_shape = jax.ShapeDtypeStruct((b, s, d), F32)
    proj_base, q_chunks = None, ()
    if proj is not None:
        g_mix, w_in_t, proj_base, q_chunks = proj
        n = w_in_t.shape[0]
        in_specs += [_resident((1, d), lambda i, j: (0, 0)), _resident((n, d), lambda i, j: (0, 0))]
        args += [g_mix.reshape(1, d), w_in_t]
        out_specs = (out_specs, pl.BlockSpec((1, tm // ATT_TILE, n, ATT_TILE), lambda i, j: (i, j, 0, 0)))
        out_shape = (out_shape, jax.ShapeDtypeStruct((b, s // ATT_TILE, n, ATT_TILE), BF16))
    if final:
        in_specs.append(_resident((1, d), lambda i, j: (0, 0)))
        args.append(g_final.reshape(1, d))
    return pl.pallas_call(
        functools.partial(_ffn_kernel, mod_base=mod_base, mixer_base=mixer_base, proj_base=proj_base,
                          q_chunks=q_chunks, final=final),
        grid=(b, s // tm),
        in_specs=in_specs,
        out_specs=out_specs,
        out_shape=out_shape,
        compiler_params=_cparams(2),
        name="ffn_mixer_out" if mixer is not None else "ffn_qkv" if proj is not None else "ffn",
    )(*args)


def _head_rmsnorm_t(o, g_col):
    return (o * lax.rsqrt(jnp.mean(o * o, axis=0, keepdims=True) + EPS)) * g_col


_CONTRACT_ROWS = (((0,), (0,)), ((), ()))


def _sb_program(q_ref, k_ref, v_ref, u_ref, g_ref, o_ref):
    nt, t = q_ref.shape[1], q_ref.shape[3]
    n_heads = q_ref.shape[2] // HEAD_DIM
    ks = u_ref.shape[0]
    sub = t // ks
    ucat = u_ref[...]
    tiles = [(h, qi, kj) for h in range(n_heads) for qi in range(nt) for kj in range(qi, -1, -1)]
    subs = range(sub - 1, -1, -1)
    z_of, cat_of, incl_of, w_of = {}, {}, {}, {}
    state = {}

    def head_rows(h):
        return slice(h * HEAD_DIM, (h + 1) * HEAD_DIM)

    def first_lane(qi, kj, si):
        return si * ks if kj == qi else 0

    def causal(lanes):
        return lax.broadcasted_iota(jnp.int32, (ks, lanes), 0) < lax.broadcasted_iota(jnp.int32, (ks, lanes), 1)

    def scores(n):
        h, qi, kj = tiles[n]
        q_t = q_ref[0, qi, head_rows(h), :]
        z_of[n] = [lax.dot_general(k_ref[0, kj, head_rows(h), si * ks:(si + 1) * ks],
                                   q_t[:, first_lane(qi, kj, si):], _CONTRACT_ROWS,
                                   preferred_element_type=F32) for si in range(sub)]

    def softplus(n):
        h, qi, kj = tiles[n]
        cats = []
        for z in z_of[n]:
            sp = jnp.maximum(z, jnp.log(1.0 + jnp.exp2(jnp.minimum(z, SOFTPLUS_LINEAR_FROM))) * LOG2E)
            if kj == qi:
                sp = jnp.where(causal(z.shape[1]), sp, 0.0)
            cats.append(sp.astype(BF16))
        cat_of[n] = cats

    def cumsum(n):
        incl_of[n] = [jnp.dot(ucat, c, preferred_element_type=F32) for c in cat_of.pop(n)]

    def weights(n):
        h, qi, kj = tiles[n]
        carry = state.setdefault((h, qi), [None, None])[0]
        zs, incls = z_of.pop(n), incl_of.pop(n)
        parts = [None] * sub
        for si in subs:
            skipped = first_lane(qi, kj, si)
            arg = zs[si] - incls[si]
            if carry is not None:
                arg = arg - carry[:, skipped:]
            w = jnp.exp2(arg)
            total = incls[si][0:1, :]
            if kj == qi:
                w = jnp.where(causal(w.shape[1]), w, 0.0)
            if skipped:
                w = jnp.concatenate([jnp.zeros((ks, skipped), F32), w], axis=1)
                total = jnp.concatenate([jnp.zeros((1, skipped), F32), total], axis=1)
            parts[si] = w.astype(BF16)
            carry = total if carry is None else carry + total
        state[h, qi][0] = carry
        w_of[n] = jnp.concatenate(parts, axis=0)

    def values(n):
        h, qi, kj = tiles[n]
        pv = jnp.dot(v_ref[0, kj, head_rows(h), :], w_of.pop(n), preferred_element_type=F32)
        acc = state[h, qi][1]
        acc = pv if acc is None else acc + pv
        state[h, qi][1] = acc
        if kj == 0:
            o_ref[0, qi, head_rows(h), :] = _head_rmsnorm_t(acc, g_ref[h]).astype(BF16)
            del state[h, qi]

    n_tiles = len(tiles)

    lead_a, lead_b = SB_STAGE_LEADS

    def step(s):
        if s < n_tiles:
            scores(s)
        if 0 <= s - lead_a < n_tiles:
            softplus(s - lead_a)
            cumsum(s - lead_a)
        if 0 <= s - lead_a - lead_b < n_tiles:
            weights(s - lead_a - lead_b)
            values(s - lead_a - lead_b)

    return [functools.partial(step, s) for s in range(n_tiles + lead_a + lead_b)]


def _suffix_matrix(n):
    return jnp.asarray(np.arange(n)[None, :] >= np.arange(n)[:, None], dtype=BF16)


def _t5_causal_bucket(n, n_buckets):
    max_exact = n_buckets // 2
    nf = np.maximum(n, 1).astype(np.float32)
    large = max_exact + (np.log(nf / max_exact) / math.log(MAX_DISTANCE / max_exact)
                         * (n_buckets - max_exact)).astype(np.int32)
    large = np.minimum(large, n_buckets - 1)
    return np.where(n < max_exact, n, large).astype(np.int32)


def _dilated_bias_tiles(rel_bias, seq, t):
    dist = np.arange(seq)
    mult = np.zeros(seq, np.float32)
    for window, dilation in DIL_CONFIGS:
        mult += ((dist % dilation == 0) & (dist <= window)).astype(np.float32)
    bucket = _t5_causal_bucket(dist, rel_bias.shape[0])
    per_dist = (rel_bias[bucket].astype(F32) + jnp.log(jnp.maximum(jnp.asarray(mult), 1.0))[:, None]) * LOG2E
    per_dist = jnp.where(jnp.asarray(mult > 0)[:, None], per_dist, NEG_INF)
    nh = rel_bias.shape[1]
    table = jnp.concatenate([jnp.full((nh, t), NEG_INF, F32), per_dist.T], axis=1).reshape(nh, 1, seq + t)
    return pl.pallas_call(
        _bias_tile_kernel,
        grid=(nh,),
        in_specs=[pl.BlockSpec((1, 1, seq + t), lambda h: (h, 0, 0))],
        out_specs=pl.BlockSpec((1, seq // t, t, t), lambda h: (h, 0, 0, 0)),
        out_shape=jax.ShapeDtypeStruct((nh, seq // t, t, t), F32),
        compiler_params=_cparams(1),
        name="dilated_bias_tiles",
    )(table)


def _bias_tile_kernel(tab_ref, o_ref):
    no, t = o_ref.shape[1], o_ref.shape[2]
    for o in range(no):
        window = jnp.broadcast_to(tab_ref[0, :, o * t:(o + 2) * t], (t, 2 * t))
        skewed = pltpu.roll(window, 0, 1, stride=1, stride_axis=0)
        o_ref[0, o] = skewed[:, t:]


def _dil_fast_program(q_ref, k_ref, v_ref, bias_ref, g_ref, o_ref):
    nt, t = q_ref.shape[1], q_ref.shape[3]
    n_heads = q_ref.shape[2] // HEAD_DIM
    ones_rows = jnp.ones((BF16_SUBLANES, t), BF16)
    tiles = [(h, qi, kj) for h in range(n_heads) for qi in range(nt) for kj in range(qi, -1, -1)]
    z_of, state = {}, {}
    bad = [jnp.zeros((HEAD_DIM + BF16_SUBLANES, t), jnp.int32)]

    def head_rows(h):
        return slice(h * HEAD_DIM, (h + 1) * HEAD_DIM)

    def scores(n):
        h, qi, kj = tiles[n]
        z = lax.dot_general(k_ref[0, kj, head_rows(h), :], q_ref[0, qi, head_rows(h), :], _CONTRACT_ROWS,
                            preferred_element_type=F32)
        z_of[n] = z + bias_ref[h, qi - kj]

    def attend(n):
        h, qi, kj = tiles[n]
        z = z_of.pop(n)
        if kj == qi:
            state[h, qi] = [jnp.max(z, axis=0, keepdims=True), None]
        m, acc = state[h, qi]
        v_ext = jnp.concatenate([v_ref[0, kj, head_rows(h), :], ones_rows], axis=0)
        pv = jnp.dot(v_ext, jnp.exp2(z - m).astype(BF16), preferred_element_type=F32)
        acc = pv if acc is None else acc + pv
        state[h, qi][1] = acc
        if kj == 0:
            bad[0] = jnp.maximum(bad[0], jnp.where(jnp.isfinite(acc), 0, 1))
            out = acc[:HEAD_DIM] / acc[HEAD_DIM:HEAD_DIM + 1]
            o_ref[0, qi, head_rows(h), :] = _head_rmsnorm_t(out, g_ref[h]).astype(BF16)
            del state[h, qi]

    def step(s):
        if s < len(tiles):
            scores(s)
        if s >= DIL_SCORE_LEAD:
            attend(s - DIL_SCORE_LEAD)

    def overflowed():
        return jnp.max(bad[0])

    return [functools.partial(step, s) for s in range(len(tiles) + DIL_SCORE_LEAD)], overflowed


def _dil_exact_program(q_ref, k_ref, v_ref, bias_ref, g_ref, o_ref, z_ref):
    nt, t = q_ref.shape[1], q_ref.shape[3]
    n_heads = q_ref.shape[2] // HEAD_DIM
    units = [(h, qi) for h in range(n_heads) for qi in range(nt)]
    ones_rows = jnp.ones((BF16_SUBLANES, t), BF16)
    m_of, acc_of = {}, {}

    def head_rows(h):
        return slice(h * HEAD_DIM, (h + 1) * HEAD_DIM)

    def score_tile(u, kj):
        h, qi = units[u]
        z = lax.dot_general(k_ref[0, kj, head_rows(h), :], q_ref[0, qi, head_rows(h), :], _CONTRACT_ROWS,
                            preferred_element_type=F32)
        z = z + bias_ref[h, qi - kj]
        z_ref[u % 2, kj] = z
        zm = jnp.max(z, axis=0, keepdims=True)
        m_of[u] = zm if kj == 0 else jnp.maximum(m_of[u], zm)

    def attend_tile(u, kj):
        h, qi = units[u]
        p = jnp.exp2(z_ref[u % 2, kj] - m_of[u]).astype(BF16)
        v_ext = jnp.concatenate([v_ref[0, kj, head_rows(h), :], ones_rows], axis=0)
        pv = jnp.dot(v_ext, p, preferred_element_type=F32)
        acc = pv if kj == 0 else acc_of[u] + pv
        acc_of[u] = acc
        if kj == qi:
            out = acc[:HEAD_DIM] / acc[HEAD_DIM:HEAD_DIM + 1]
            o_ref[0, qi, head_rows(h), :] = _head_rmsnorm_t(out, g_ref[h]).astype(BF16)
            del acc_of[u], m_of[u]

    steps = [functools.partial(score_tile, 0, kj) for kj in range(units[0][1] + 1)]
    for u, (_, qi) in enumerate(units):
        if u + 1 < len(units):
            steps += [functools.partial(score_tile, u + 1, kj) for kj in range(units[u + 1][1] + 1)]
        steps += [functools.partial(attend_tile, u, kj) for kj in range(qi + 1)]
    return steps


def _attention_kernel(qa_ref, ka_ref, va_ref, qb_ref, kb_ref, vb_ref, u_ref, bias_ref, ga_ref, gb_ref,
                      oa_ref, ob_ref, z_ref):
    sb_steps = _sb_program(qa_ref, ka_ref, va_ref, u_ref, ga_ref, oa_ref)
    dil_steps, dil_overflowed = _dil_fast_program(qb_ref, kb_ref, vb_ref, bias_ref, gb_ref, ob_ref)
    n = max(len(sb_steps), len(dil_steps))
    done_sb = done_dil = 0
    for i in range(1, n + 1):
        for step in sb_steps[done_sb:len(sb_steps) * i // n]:
            step()
        for step in dil_steps[done_dil:len(dil_steps) * i // n]:
            step()
        done_sb, done_dil = len(sb_steps) * i // n, len(dil_steps) * i // n

    @pl.when(dil_overflowed() > 0)
    def _():
        for step in _dil_exact_program(qb_ref, kb_ref, vb_ref, bias_ref, gb_ref, ob_ref, z_ref):
            step()


def _attention(qkv_t, bias_tiles, g_sb, g_dil, sb_blk, dil_blk):
    b, nt, _, t = qkv_t.shape
    nh = g_sb.shape[0]
    assert g_dil.shape[0] == nh
    hps = HEADS_PER_STEP
    blk = (1, nt, hps * HEAD_DIM, t)

    def rows(first_block):
        return pl.BlockSpec(blk, lambda h, i: (i, 0, first_block // hps + h, 0))

    heads = pl.BlockSpec((hps, HEAD_DIM, 1), lambda h, i: (h, 0, 0))
    out_shape = jax.ShapeDtypeStruct((b, nt, nh * HEAD_DIM, t), BF16)
    return pl.pallas_call(
        _attention_kernel,
        grid=(nh // hps, b),
        in_specs=[rows(r) for r in sb_blk] + [rows(r) for r in dil_blk] + [
            _resident((SB_SUB_TILE, SB_SUB_TILE), lambda h, i: (0, 0)),
            pl.BlockSpec((hps, nt, t, t), lambda h, i: (h, 0, 0, 0)),
            heads, heads,
        ],
        out_specs=(pl.BlockSpec(blk, lambda h, i: (i, 0, h, 0)),) * 2,
        out_shape=(out_shape, out_shape),
        scratch_shapes=[pltpu.VMEM((2, nt, t, t), F32)],
        compiler_params=_cparams(2),
        name="attention",
    )(*([qkv_t] * 6), _suffix_matrix(SB_SUB_TILE), bias_tiles,
      g_sb.reshape(nh, HEAD_DIM, 1), g_dil.reshape(nh, HEAD_DIM, 1))


def kernel(x, c, w_ada, b_ada, g_ffn1, w1_gate, w1_up, w1_down, g_mix, w_in, g_sb_out, g_dil_out, w_out, rel_bias, g_ffn2, w2_gate, w2_up, w2_down, g_final):
    depth = w_ada.shape[0]
    seq = x.shape[1]
    nh_sb = g_sb_out.shape[1]
    nh_dil = g_dil_out.shape[1]
    d_sb = nh_sb * HEAD_DIM
    sb_blk = (0, nh_sb, 2 * nh_sb)
    dil_blk = (3 * nh_sb, 3 * nh_sb + nh_dil, 3 * nh_sb + 2 * nh_dil)
    d_dil = nh_dil * HEAD_DIM
    assert d_sb % PROJ_ROW_CHUNK == 0 and d_dil % PROJ_ROW_CHUNK == 0
    assert g_sb_out.shape[2] == HEAD_DIM and g_dil_out.shape[2] == HEAD_DIM and w_in.shape[2] == 3 * (d_sb + d_dil)
    assert seq % ATT_TILE == 0 and seq <= MAX_DISTANCE and nh_sb % HEADS_PER_STEP == 0
    q_chunks = tuple(range(d_sb // PROJ_ROW_CHUNK)) + tuple(
        range(3 * d_sb // PROJ_ROW_CHUNK, (3 * d_sb + d_dil) // PROJ_ROW_CHUNK))
    bias_tiles = _dilated_bias_tiles(rel_bias, seq, ATT_TILE)
    for l in range(depth):
        mod = _adaln_mod(c, w_ada[l], b_ada[l])
        x, qkv_t = _ffn(x, mod, g_ffn1[l], w1_gate[l].astype(BF16), w1_up[l].astype(BF16), w1_down[l].astype(BF16), 0,
                        proj=(g_mix[l], w_in[l].T.astype(BF16), 3, q_chunks))
        o_sb, o_dil = _attention(qkv_t, bias_tiles, g_sb_out[l], g_dil_out[l], sb_blk, dil_blk)
        w_o = w_out[l].astype(BF16)
        last = l == depth - 1
        x = _ffn(x, mod, g_ffn2[l], w2_gate[l].astype(BF16), w2_up[l].astype(BF16), w2_down[l].astype(BF16), 6,
                 mixer=(o_sb, o_dil, w_o[:d_sb], w_o[d_sb:], 3), g_final=g_final if last else None)
    return x
```

```python
import functools
import math

import jax
import jax.numpy as jnp
import numpy as np
from jax import lax
from jax.experimental import pallas as pl
from jax.experimental.pallas import tpu as pltpu

F32 = jnp.float32
BF16 = jnp.bfloat16

HEAD_DIM = 64
DIL_CONFIGS = ((128, 1), (512, 4), (2048, 16))
MAX_DISTANCE = 2048
N_MOD = 9
EPS = 1e-6
NEG_INF = -1e30

ATT_TILE = 256
SB_SUB_TILE = 64
HEADS_PER_STEP = 2
SOFTPLUS_LINEAR_FROM = 64.0
SB_STAGE_LEADS = (1, 1)
DIL_SCORE_LEAD = 4
LOG2E = 1.4426950408889634
BF16_SUBLANES = 16
LANES = 128
FFN_TOKEN_TILE = 512
FFN_OUT_TOKEN_TILE = 1024
PROJ_ROW_CHUNK = 512
MOD_COL_TILE = 2304
VMEM_LIMIT = 56 * 1024 * 1024


def _cparams(n_grid):
    return pltpu.CompilerParams(dimension_semantics=("arbitrary",) * n_grid, vmem_limit_bytes=VMEM_LIMIT)


def _resident(block_shape, index_map):
    return pl.BlockSpec(block_shape, index_map, pipeline_mode=pl.Buffered(1))


def _mod_kernel(c_ref, w_ref, b_ref, o_ref):
    c = c_ref[...]
    s = c * jax.nn.sigmoid(c)
    o_ref[...] = jnp.dot(s.astype(BF16), w_ref[...].astype(BF16), preferred_element_type=F32) + b_ref[...]


def _adaln_mod(c, w_ada, b_ada):
    b, d = c.shape
    n = w_ada.shape[1]
    tn = MOD_COL_TILE
    assert n % tn == 0 and n == N_MOD * d
    out = pl.pallas_call(
        _mod_kernel,
        grid=(n // tn,),
        in_specs=[
            pl.BlockSpec((b, d), lambda j: (0, 0)),
            pl.BlockSpec((d, tn), lambda j: (0, j)),
            pl.BlockSpec((1, tn), lambda j: (0, j)),
        ],
        out_specs=pl.BlockSpec((b, tn), lambda j: (0, j)),
        out_shape=jax.ShapeDtypeStruct((b, n), F32),
        compiler_params=_cparams(1),
        name="adaln_mod",
    )(c, w_ada, b_ada.reshape(1, n))
    return out.reshape(b, N_MOD, d)


def _modulate(x, g, shift, scale):
    y = x * lax.rsqrt(jnp.mean(x * x, axis=-1, keepdims=True) + EPS)
    return (y * g) * (1.0 + scale) + shift


def _ffn_kernel(*refs, mod_base, mixer_base, proj_base, q_chunks, final):
    x_ref, mod_ref, g_ref, wg_ref, wu_ref, wd_ref = refs[:6]
    rest = list(refs[6:])
    qkv_ref = rest.pop() if proj_base is not None else None
    o_ref = rest.pop()
    gf_ref = rest.pop() if final else None
    gmix_ref, win_ref = (rest.pop(-2), rest.pop()) if proj_base is not None else (None, None)
    mixer_refs = rest if mixer_base is not None else None
    shift = mod_ref[0, mod_base:mod_base + 1, :]
    scale = mod_ref[0, mod_base + 1:mod_base + 2, :]
    gate = mod_ref[0, mod_base + 2:mod_base + 3, :]
    sub = ATT_TILE
    n_sub = x_ref.shape[1] // sub
    xs, hs, acts = {}, {}, {}

    def prologue(i):
        x = x_ref[0, i * sub:(i + 1) * sub, :]
        if mixer_refs is not None:
            osb_ref, odil_ref, wsb_ref, wdil_ref = mixer_refs
            y = lax.dot_general(osb_ref[0, i], wsb_ref[...], _CONTRACT_ROWS, preferred_element_type=F32)
            y = y + lax.dot_general(odil_ref[0, i], wdil_ref[...], _CONTRACT_ROWS, preferred_element_type=F32)
            x = x + mod_ref[0, mixer_base + 2:mixer_base + 3, :] * y
        xs[i] = x
        hs[i] = _modulate(x, g_ref[...], shift, scale).astype(BF16)

    def expand(i):
        hb = hs.pop(i)
        gg = jnp.dot(hb, wg_ref[...], preferred_element_type=F32)
        uu = jnp.dot(hb, wu_ref[...], preferred_element_type=F32)
        acts[i] = ((gg * jax.nn.sigmoid(gg)) * uu).astype(BF16)

    def contract(i):
        out = xs.pop(i) + (0.5 * gate) * jnp.dot(acts.pop(i), wd_ref[...], preferred_element_type=F32)
        if proj_base is not None:
            xs[i] = out
        if final:
            out = out * lax.rsqrt(jnp.mean(out * out, axis=-1, keepdims=True) + EPS) * gf_ref[...]
        o_ref[0, i * sub:(i + 1) * sub, :] = out

    def project(i):
        hb = _modulate(xs.pop(i), gmix_ref[...], mod_ref[0, proj_base:proj_base + 1, :],
                       mod_ref[0, proj_base + 1:proj_base + 2, :]).astype(BF16)
        q_scale = HEAD_DIM ** -0.5 * LOG2E
        for r in range(win_ref.shape[0] // PROJ_ROW_CHUNK):
            rows = slice(r * PROJ_ROW_CHUNK, (r + 1) * PROJ_ROW_CHUNK)
            res = lax.dot_general(win_ref[rows, :], hb, (((1,), (1,)), ((), ())), preferred_element_type=F32)
            if r in q_chunks:
                res = res * q_scale
            qkv_ref[0, i, rows, :] = res.astype(BF16)

    for i in range(n_sub):
        prologue(i)
    for i in range(n_sub):
        expand(i)
    for i in range(n_sub):
        contract(i)
    if proj_base is not None:
        for i in range(n_sub):
            project(i)


def _ffn(x, mod, g, wg, wu, wd, mod_base, mixer=None, proj=None, g_final=None):
    b, s, d = x.shape
    dff = wg.shape[1]
    tm = FFN_TOKEN_TILE if proj is not None else FFN_OUT_TOKEN_TILE
    assert s % tm == 0 and tm % ATT_TILE == 0
    final = g_final is not None
    in_specs = [
        pl.BlockSpec((1, tm, d), lambda i, j: (i, j, 0)),
        pl.BlockSpec((1, N_MOD, d), lambda i, j: (i, 0, 0)),
        _resident((1, d), lambda i, j: (0, 0)),
        _resident((d, dff), lambda i, j: (0, 0)),
        _resident((d, dff), lambda i, j: (0, 0)),
        _resident((dff, d), lambda i, j: (0, 0)),
    ]
    args = [x, mod, g.reshape(1, d), wg, wu, wd]
    mixer_base = None
    if mixer is not None:
        o_sb, o_dil, w_sb, w_dil, mixer_base = mixer
        t = o_sb.shape[3]
        for o_heads in (o_sb, o_dil):
            in_specs.append(pl.BlockSpec((1, tm // t, o_heads.shape[2], t), lambda i, j: (i, j, 0, 0)))
            args.append(o_heads)
        for w_heads in (w_sb, w_dil):
            in_specs.append(_resident(w_heads.shape, lambda i, j: (0, 0)))
            args.append(w_heads)
    out_specs = pl.BlockSpec((1, tm, d), lambda i, j: (i, j, 0))
    out_shape = jax.ShapeDtypeStruct((b, s, d), F32)
    proj_base, q_chunks = None, ()
    if proj is not None:
        g_mix, w_in_t, proj_base, q_chunks = proj
        n = w_in_t.shape[0]
        in_specs += [_resident((1, d), lambda i, j: (0, 0)), _resident((n, d), lambda i, j: (0, 0))]
        args += [g_mix.reshape(1, d), w_in_t]
        out_specs = (out_specs, pl.BlockSpec((1, tm // ATT_TILE, n, ATT_TILE), lambda i, j: (i, j, 0, 0)))
        out_shape = (out_shape, jax.ShapeDtypeStruct((b, s // ATT_TILE, n, ATT_TILE), BF16))
    if final:
        in_specs.append(_resident((1, d), lambda i, j: (0, 0)))
        args.append(g_final.reshape(1, d))
    return pl.pallas_call(
        functools.partial(_ffn_kernel, mod_base=mod_base, mixer_base=mixer_base, proj_base=proj_base,
                          q_chunks=q_chunks, final=final),
        grid=(b, s // tm),
        in_specs=in_specs,
        out_specs=out_specs,
        out_shape=out_shape,
        compiler_params=_cparams(2),
        name="ffn_mixer_out" if mixer is not None else "ffn_qkv" if proj is not None else "ffn",
    )(*args)


def _head_rmsnorm_t(o, g_col):
    return (o * lax.rsqrt(jnp.mean(o * o, axis=0, keepdims=True) + EPS)) * g_col


_CONTRACT_ROWS = (((0,), (0,)), ((), ()))


def _sb_program(q_ref, k_ref, v_ref, u_ref, g_ref, o_ref):
    nt, t = q_ref.shape[1], q_ref.shape[3]
    n_heads = q_ref.shape[2] // HEAD_DIM
    ks = u_ref.shape[0]
    sub = t // ks
    ucat = u_ref[...]
    tiles = [(h, qi, kj) for h in range(n_heads) for qi in range(nt) for kj in range(qi, -1, -1)]
    subs = range(sub - 1, -1, -1)
    z_of, cat_of, incl_of, w_of = {}, {}, {}, {}
    state = {}

    def head_rows(h):
        return slice(h * HEAD_DIM, (h + 1) * HEAD_DIM)

    def first_lane(qi, kj, si):
        return si * ks // LANES * LANES if kj == qi else 0

    def causal(lanes, si):
        ahead = (t - lanes) - si * ks
        return (lax.broadcasted_iota(jnp.int32, (ks, lanes), 0)
                < lax.broadcasted_iota(jnp.int32, (ks, lanes), 1) + ahead)

    def scores(n):
        h, qi, kj = tiles[n]
        q_t = q_ref[0, qi, head_rows(h), :]
        z_of[n] = [lax.dot_general(k_ref[0, kj, head_rows(h), si * ks:(si + 1) * ks],
                                   q_t[:, first_lane(qi, kj, si):], _CONTRACT_ROWS,
                                   preferred_element_type=F32) for si in range(sub)]

    def softplus(n):
        h, qi, kj = tiles[n]
        cats = []
        for si, z in enumerate(z_of[n]):
            sp = jnp.maximum(z, jnp.log(1.0 + jnp.exp2(jnp.minimum(z, SOFTPLUS_LINEAR_FROM))) * LOG2E)
            if kj == qi:
                sp = jnp.where(causal(z.shape[1], si), sp, 0.0)
            cats.append(sp.astype(BF16))
        cat_of[n] = cats

    def cumsum(n):
        incl_of[n] = [jnp.dot(ucat, c, preferred_element_type=F32) for c in cat_of.pop(n)]

    def weights(n):
        h, qi, kj = tiles[n]
        carry = state.setdefault((h, qi), [None, None])[0]
        zs, incls = z_of.pop(n), incl_of.pop(n)
        parts = [None] * sub
        for si in subs:
            skipped = first_lane(qi, kj, si)
            arg = zs[si] - incls[si]
            if carry is not None:
                arg = arg - carry[:, skipped:]
            w = jnp.exp2(arg)
            total = incls[si][0:1, :]
            if kj == qi:
                w = jnp.where(causal(w.shape[1], si), w, 0.0)
            if skipped:
                w = jnp.concatenate([jnp.zeros((ks, skipped), F32), w], axis=1)
                total = jnp.concatenate([jnp.zeros((1, skipped), F32), total], axis=1)
            parts[si] = w.astype(BF16)
            carry = total if carry is None else carry + total
        state[h, qi][0] = carry
        w_of[n] = jnp.concatenate(parts, axis=0)

    def values(n):
        h, qi, kj = tiles[n]
        pv = jnp.dot(v_ref[0, kj, head_rows(h), :], w_of.pop(n), preferred_element_type=F32)
        acc = state[h, qi][1]
        acc = pv if acc is None else acc + pv
        state[h, qi][1] = acc
        if kj == 0:
            o_ref[0, qi, head_rows(h), :] = _head_rmsnorm_t(acc, g_ref[h]).astype(BF16)
            del state[h, qi]

    n_tiles = len(tiles)

    lead_a, lead_b = SB_STAGE_LEADS

    def step(s):
        if s < n_tiles:
            scores(s)
        if 0 <= s - lead_a < n_tiles:
            softplus(s - lead_a)
            cumsum(s - lead_a)
        if 0 <= s - lead_a - lead_b < n_tiles:
            weights(s - lead_a - lead_b)
            values(s - lead_a - lead_b)

    return [functools.partial(step, s) for s in range(n_tiles + lead_a + lead_b)]


def _suffix_matrix(n):
    return jnp.asarray(np.arange(n)[None, :] >= np.arange(n)[:, None], dtype=BF16)


def _t5_causal_bucket(n, n_buckets):
    max_exact = n_buckets // 2
    nf = np.maximum(n, 1).astype(np.float32)
    large = max_exact + (np.log(nf / max_exact) / math.log(MAX_DISTANCE / max_exact)
                         * (n_buckets - max_exact)).astype(np.int32)
    large = np.minimum(large, n_buckets - 1)
    return np.where(n < max_exact, n, large).astype(np.int32)


def _dilated_bias_tiles(rel_bias, seq, t):
    dist = np.arange(seq)
    mult = np.zeros(seq, np.float32)
    for window, dilation in DIL_CONFIGS:
        mult += ((dist % dilation == 0) & (dist <= window)).astype(np.float32)
    bucket = _t5_causal_bucket(dist, rel_bias.shape[0])
    per_dist = (rel_bias[bucket].astype(F32) + jnp.log(jnp.maximum(jnp.asarray(mult), 1.0))[:, None]) * LOG2E
    per_dist = jnp.where(jnp.asarray(mult > 0)[:, None], per_dist, NEG_INF)
    nh = rel_bias.shape[1]
    table = jnp.concatenate([jnp.full((nh, t), NEG_INF, F32), per_dist.T], axis=1).reshape(nh, 1, seq + t)
    return pl.pallas_call(
        _bias_tile_kernel,
        grid=(nh,),
        in_specs=[pl.BlockSpec((1, 1, seq + t), lambda h: (h, 0, 0))],
        out_specs=pl.BlockSpec((1, seq // t, t, t), lambda h: (h, 0, 0, 0)),
        out_shape=jax.ShapeDtypeStruct((nh, seq // t, t, t), F32),
        compiler_params=_cparams(1),
        name="dilated_bias_tiles",
    )(table)


def _bias_tile_kernel(tab_ref, o_ref):
    no, t = o_ref.shape[1], o_ref.shape[2]
    for o in range(no):
        window = jnp.broadcast_to(tab_ref[0, :, o * t:(o + 2) * t], (t, 2 * t))
        skewed = pltpu.roll(window, 0, 1, stride=1, stride_axis=0)
        o_ref[0, o] = skewed[:, t:]


def _dil_fast_program(q_ref, k_ref, v_ref, bias_ref, g_ref, o_ref):
    nt, t = q_ref.shape[1], q_ref.shape[3]
    n_heads = q_ref.shape[2] // HEAD_DIM
    ones_rows = jnp.ones((BF16_SUBLANES, t), BF16)
    tiles = [(h, qi, kj) for h in range(n_heads) for qi in range(nt) for kj in range(qi, -1, -1)]
    z_of, state = {}, {}
    bad = [jnp.zeros((HEAD_DIM + BF16_SUBLANES, t), jnp.int32)]

    def head_rows(h):
        return slice(h * HEAD_DIM, (h + 1) * HEAD_DIM)

    def scores(n):
        h, qi, kj = tiles[n]
        z = lax.dot_general(k_ref[0, kj, head_rows(h), :], q_ref[0, qi, head_rows(h), :], _CONTRACT_ROWS,
                            preferred_element_type=F32)
        z_of[n] = z + bias_ref[h, qi - kj]

    def attend(n):
        h, qi, kj = tiles[n]
        z = z_of.pop(n)
        if kj == qi:
            state[h, qi] = [jnp.max(z, axis=0, keepdims=True), None]
        m, acc = state[h, qi]
        v_ext = jnp.concatenate([v_ref[0, kj, head_rows(h), :], ones_rows], axis=0)
        pv = jnp.dot(v_ext, jnp.exp2(z - m).astype(BF16), preferred_element_type=F32)
        acc = pv if acc is None else acc + pv
        state[h, qi][1] = acc
        if kj == 0:
            bad[0] = jnp.maximum(bad[0], jnp.where(jnp.isfinite(acc), 0, 1))
            out = acc[:HEAD_DIM] / acc[HEAD_DIM:HEAD_DIM + 1]
            o_ref[0, qi, head_rows(h), :] = _head_rmsnorm_t(out, g_ref[h]).astype(BF16)
            del state[h, qi]

    def step(s):
        if s < len(tiles):
            scores(s)
        if s >= DIL_SCORE_LEAD:
            attend(s - DIL_SCORE_LEAD)

    def overflowed():
        return jnp.max(bad[0])

    return [functools.partial(step, s) for s in range(len(tiles) + DIL_SCORE_LEAD)], overflowed


def _dil_exact_program(q_ref, k_ref, v_ref, bias_ref, g_ref, o_ref, z_ref):
    nt, t = q_ref.shape[1], q_ref.shape[3]
    n_heads = q_ref.shape[2] // HEAD_DIM
    units = [(h, qi) for h in range(n_heads) for qi in range(nt)]
    ones_rows = jnp.ones((BF16_SUBLANES, t), BF16)
    m_of, acc_of = {}, {}

    def head_rows(h):
        return slice(h * HEAD_DIM, (h + 1) * HEAD_DIM)

    def score_tile(u, kj):
        h, qi = units[u]
        z = lax.dot_general(k_ref[0, kj, head_rows(h), :], q_ref[0, qi, head_rows(h), :], _CONTRACT_ROWS,
                            preferred_element_type=F32)
        z = z + bias_ref[h, qi - kj]
        z_ref[u % 2, kj] = z
        zm = jnp.max(z, axis=0, keepdims=True)
        m_of[u] = zm if kj == 0 else jnp.maximum(m_of[u], zm)

    def attend_tile(u, kj):
        h, qi = units[u]
        p = jnp.exp2(z_ref[u % 2, kj] - m_of[u]).astype(BF16)
        v_ext = jnp.concatenate([v_ref[0, kj, head_rows(h), :], ones_rows], axis=0)
        pv = jnp.dot(v_ext, p, preferred_element_type=F32)
        acc = pv if kj == 0 else acc_of[u] + pv
        acc_of[u] = acc
        if kj == qi:
            out = acc[:HEAD_DIM] / acc[HEAD_DIM:HEAD_DIM + 1]
            o_ref[0, qi, head_rows(h), :] = _head_rmsnorm_t(out, g_ref[h]).astype(BF16)
            del acc_of[u], m_of[u]

    steps = [functools.partial(score_tile, 0, kj) for kj in range(units[0][1] + 1)]
    for u, (_, qi) in enumerate(units):
        if u + 1 < len(units):
            steps += [functools.partial(score_tile, u + 1, kj) for kj in range(units[u + 1][1] + 1)]
        steps += [functools.partial(attend_tile, u, kj) for kj in range(qi + 1)]
    return steps


def _attention_kernel(qa_ref, ka_ref, va_ref, qb_ref, kb_ref, vb_ref, u_ref, bias_ref, ga_ref, gb_ref,
                      oa_ref, ob_ref, z_ref):
    sb_steps = _sb_program(qa_ref, ka_ref, va_ref, u_ref, ga_ref, oa_ref)
    dil_steps, dil_overflowed = _dil_fast_program(qb_ref, kb_ref, vb_ref, bias_ref, gb_ref, ob_ref)
    n = max(len(sb_steps), len(dil_steps))
    done_sb = done_dil = 0
    for i in range(1, n + 1):
        for step in sb_steps[done_sb:len(sb_steps) * i // n]:
            step()
        for step in dil_steps[done_dil:len(dil_steps) * i // n]:
            step()
        done_sb, done_dil = len(sb_steps) * i // n, len(dil_steps) * i // n

    @pl.when(dil_overflowed() > 0)
    def _():
        for step in _dil_exact_program(qb_ref, kb_ref, vb_ref, bias_ref, gb_ref, ob_ref, z_ref):
            step()


def _attention(qkv_t, bias_tiles, g_sb, g_dil, sb_blk, dil_blk):
    b, nt, _, t = qkv_t.shape
    nh = g_sb.shape[0]
    assert g_dil.shape[0] == nh
    hps = HEADS_PER_STEP
    blk = (1, nt, hps * HEAD_DIM, t)

    def rows(first_block):
        return pl.BlockSpec(blk, lambda h, i: (i, 0, first_block // hps + h, 0))

    heads = pl.BlockSpec((hps, HEAD_DIM, 1), lambda h, i: (h, 0, 0))
    out_shape = jax.ShapeDtypeStruct((b, nt, nh * HEAD_DIM, t), BF16)
    return pl.pallas_call(
        _attention_kernel,
        grid=(nh // hps, b),
        in_specs=[rows(r) for r in sb_blk] + [rows(r) for r in dil_blk] + [
            _resident((SB_SUB_TILE, SB_SUB_TILE), lambda h, i: (0, 0)),
            pl.BlockSpec((hps, nt, t, t), lambda h, i: (h, 0, 0, 0)),
            heads, heads,
        ],
        out_specs=(pl.BlockSpec(blk, lambda h, i: (i, 0, h, 0)),) * 2,
        out_shape=(out_shape, out_shape),
        scratch_shapes=[pltpu.VMEM((2, nt, t, t), F32)],
        compiler_params=_cparams(2),
        name="attention",
    )(*([qkv_t] * 6), _suffix_matrix(SB_SUB_TILE), bias_tiles,
      g_sb.reshape(nh, HEAD_DIM, 1), g_dil.reshape(nh, HEAD_DIM, 1))


def kernel(x, c, w_ada, b_ada, g_ffn1, w1_gate, w1_up, w1_down, g_mix, w_in, g_sb_out, g_dil_out, w_out, rel_bias, g_ffn2, w2_gate, w2_up, w2_down, g_final):
    depth = w_ada.shape[0]
    seq = x.shape[1]
    nh_sb = g_sb_out.shape[1]
    nh_dil = g_dil_out.shape[1]
    d_sb = nh_sb * HEAD_DIM
    sb_blk = (0, nh_sb, 2 * nh_sb)
    dil_blk = (3 * nh_sb, 3 * nh_sb + nh_dil, 3 * nh_sb + 2 * nh_dil)
    d_dil = nh_dil * HEAD_DIM
    assert d_sb % PROJ_ROW_CHUNK == 0 and d_dil % PROJ_ROW_CHUNK == 0
    assert g_sb_out.shape[2] == HEAD_DIM and g_dil_out.shape[2] == HEAD_DIM and w_in.shape[2] == 3 * (d_sb + d_dil)
    assert seq % ATT_TILE == 0 and seq <= MAX_DISTANCE and nh_sb % HEADS_PER_STEP == 0
    q_chunks = tuple(range(d_sb // PROJ_ROW_CHUNK)) + tuple(
        range(3 * d_sb // PROJ_ROW_CHUNK, (3 * d_sb + d_dil) // PROJ_ROW_CHUNK))
    bias_tiles = _dilated_bias_tiles(rel_bias, seq, ATT_TILE)
    for l in range(depth):
        mod = _adaln_mod(c, w_ada[l], b_ada[l])
        x, qkv_t = _ffn(x, mod, g_ffn1[l], w1_gate[l].astype(BF16), w1_up[l].astype(BF16), w1_down[l].astype(BF16), 0,
                        proj=(g_mix[l], w_in[l].T.astype(BF16), 3, q_chunks))
        o_sb, o_dil = _attention(qkv_t, bias_tiles, g_sb_out[l], g_dil_out[l], sb_blk, dil_blk)
        w_o = w_out[l].astype(BF16)
        last = l == depth - 1
        x = _ffn(x, mod, g_ffn2[l], w2_gate[l].astype(BF16), w2_up[l].astype(BF16), w2_down[l].astype(BF16), 6,
                 mixer=(o_sb, o_dil, w_o[:d_sb], w_o[d_sb:], 3), g_final=g_final if last else None)
    return x
```
